```python
import math
import jax, jax.numpy as jnp
from jax import lax
import numpy as np

D_MODEL = 4096
BATCH = 4
SEQ = 4096
DEPTH = 4

HEAD_DIM = 128
BLOCK = 128
ROPE_THETA = 10000.0
EPS = 1e-6
SWA_Q_HEADS = 16
SWA_KV_HEADS = 4
WINDOW = 128
N_BAND = -(-WINDOW // BLOCK) + 1
SB_HEADS = 8
DIFF_HEADS = 8
DIFF_QK_DIM = HEAD_DIM // 2
DIFF_NORM_EPS = 1e-5
D_FF = 4096
N_BRANCH = 3

SWA_Q_W = SWA_Q_HEADS * HEAD_DIM
SWA_KV_W = SWA_KV_HEADS * HEAD_DIM
SB_W = SB_HEADS * HEAD_DIM
DIFF_W = DIFF_HEADS * HEAD_DIM
QKV_WIDTHS = (SWA_Q_W, SWA_KV_W, SWA_KV_W, SB_W, SB_W, SB_W, DIFF_W, DIFF_W, DIFF_W)
QKV_W = sum(QKV_WIDTHS)

kernel_name = "hybrid_gated_swa_stickbreak_diffattn_macaron"


def rms_norm(x, g, eps=EPS):
    xf = x.astype(jnp.float32)
    y = xf * lax.rsqrt(jnp.mean(xf * xf, axis=-1, keepdims=True) + eps)
    return (y * g.astype(jnp.float32)).astype(x.dtype)


def swiglu(x, w_in, w_out):
    gu = x @ w_in
    g, u = jnp.split(gu, 2, axis=-1)
    return (jax.nn.silu(g) * u) @ w_out


def rope_tables(seq, dim):
    inv = 1.0 / (ROPE_THETA ** (jnp.arange(0, dim, 2, dtype=jnp.float32) / dim))
    ang = jnp.arange(seq, dtype=jnp.float32)[:, None] * inv[None, :]
    return jnp.cos(ang), jnp.sin(ang)


def apply_rope(x, cos, sin):
    xf = x.astype(jnp.float32)
    half = xf.shape[-1] // 2
    x1, x2 = xf[..., :half], xf[..., half:]
    c, s = cos[:, None, :], sin[:, None, :]
    return jnp.concatenate([x1 * c - x2 * s, x2 * c + x1 * s], axis=-1).astype(x.dtype)


def sliding_window_gqa(q, k, v, sinks):
    b, s, hq, d = q.shape
    hkv = k.shape[2]
    g = hq // hkv
    nb = s // BLOCK
    qb = q.reshape(b, nb, BLOCK, hkv, g, d)

    def band(t):
        tb = t.reshape(b, nb, BLOCK, hkv, d)
        tp = jnp.pad(tb, ((0, 0), (N_BAND - 1, 0), (0, 0), (0, 0), (0, 0)))
        return jnp.concatenate([tp[:, j:j + nb] for j in range(N_BAND)], axis=2)

    kk, vv = band(k), band(v)
    scores = jnp.einsum('bnqhgd,bnkhd->bnhgqk', qb, kk,
                        preferred_element_type=jnp.float32) * (1.0 / math.sqrt(d))
    qi = jnp.arange(BLOCK)[:, None]
    kj = jnp.arange(N_BAND * BLOCK)[None, :]
    rel = qi - kj + (N_BAND - 1) * BLOCK
    key_pos = (jnp.arange(nb)[:, None, None] - (N_BAND - 1)) * BLOCK + kj[None]
    valid = (rel >= 0)[None] & (rel < WINDOW)[None] & (key_pos >= 0)
    scores = jnp.where(valid[None, :, None, None], scores, -jnp.inf)
    sink = jnp.broadcast_to(sinks.astype(jnp.float32).reshape(1, 1, hkv, g, 1, 1),
                            scores.shape[:-1] + (1,))
    probs = jax.nn.softmax(jnp.concatenate([scores, sink], axis=-1), axis=-1)[..., :-1]
    out = jnp.einsum('bnhgqk,bnkhd->bnqhgd', probs.astype(v.dtype), vv)
    return out.reshape(b, s, hq * d)


def stick_breaking_attention(q, k, v):
    b, s, h, d = q.shape
    nb = s // BLOCK
    qb = q.reshape(b, nb, BLOCK, h, d).transpose(1, 0, 2, 3, 4)
    key_idx = jnp.arange(s)

    def one_block(args):
        i, qblk = args
        z = jnp.einsum('bqhd,bkhd->bhqk', qblk, k,
                       preferred_element_type=jnp.float32) * (1.0 / math.sqrt(d))
        t = i * BLOCK + jnp.arange(BLOCK)
        causal = key_idx[None, :] < t[:, None]
        log_beta = jax.nn.log_sigmoid(z)
        log_1m = jnp.where(causal, jax.nn.log_sigmoid(-z), 0.0)
        after = lax.cumsum(log_1m, axis=3, reverse=True) - log_1m
        a = jnp.where(causal, jnp.exp(log_beta + after), 0.0)
        return jnp.einsum('bhqk,bkhd->bqhd', a.astype(v.dtype), v)

    out = lax.map(one_block, (jnp.arange(nb), qb))
    return out.transpose(1, 0, 2, 3, 4).reshape(b, s, h * d)


def differential_attention(q, k, v, lam):
    b, s, h, _, dh = q.shape
    nb = s // BLOCK
    qb = q.reshape(b, nb, BLOCK, h, 2, dh).transpose(1, 0, 2, 3, 4, 5)
    key_idx = jnp.arange(s)

    def one_block(args):
        i, qblk = args
        sc = jnp.einsum('bqhcd,bkhcd->bhcqk', qblk, k,
                        preferred_element_type=jnp.float32) * (1.0 / math.sqrt(dh))
        t = i * BLOCK + jnp.arange(BLOCK)
        causal = key_idx[None, :] <= t[:, None]
        p = jax.nn.softmax(jnp.where(causal, sc, -jnp.inf), axis=-1)
        w = p[:, :, 0] - lam * p[:, :, 1]
        return jnp.einsum('bhqk,bkhd->bqhd', w.astype(v.dtype), v)

    out = lax.map(one_block, (jnp.arange(nb), qb))
    return out.transpose(1, 0, 2, 3, 4).reshape(b, s, h, v.shape[-1])


def setup_inputs(seed: int = 0) -> dict:
    key = jax.random.key(seed)
    ks = iter(jax.random.split(key, 32))
    f32 = jnp.float32

    def w(shape, fan_in):
        return jax.random.normal(next(ks), shape, f32) * (fan_in ** -0.5)

    def gain(shape):
        return 1.0 + 0.02 * jax.random.normal(next(ks), shape, f32)

    return {
        "x": jax.random.normal(next(ks), (BATCH, SEQ, D_MODEL), f32),
        "ffn1_norm": gain((DEPTH, D_MODEL)),
        "ffn1_w_in": w((DEPTH, D_MODEL, 2 * D_FF), D_MODEL),
        "ffn1_w_out": w((DEPTH, D_FF, D_MODEL), D_FF),
        "mix_norm": gain((DEPTH, D_MODEL)),
        "w_qkv": w((DEPTH, D_MODEL, QKV_W), D_MODEL),
        "w_gate": w((DEPTH, D_MODEL, N_BRANCH * D_MODEL), D_MODEL),
        "sinks": 0.5 * jax.random.normal(next(ks), (DEPTH, SWA_Q_HEADS), f32),
        "lambda_q1": 0.1 * jax.random.normal(next(ks), (DEPTH, DIFF_QK_DIM), f32),
        "lambda_k1": 0.1 * jax.random.normal(next(ks), (DEPTH, DIFF_QK_DIM), f32),
        "lambda_q2": 0.1 * jax.random.normal(next(ks), (DEPTH, DIFF_QK_DIM), f32),
        "lambda_k2": 0.1 * jax.random.normal(next(ks), (DEPTH, DIFF_QK_DIM), f32),
        "diff_norm": gain((DEPTH, HEAD_DIM)),
        "w_branch_a": w((DEPTH, SWA_Q_W, D_MODEL), SWA_Q_W),
        "w_branch_b": w((DEPTH, SB_W, D_MODEL), SB_W),
        "w_branch_c": w((DEPTH, DIFF_W, D_MODEL), DIFF_W),
        "w_out": w((DEPTH, D_MODEL, D_MODEL), D_MODEL),
        "ffn2_norm": gain((DEPTH, D_MODEL)),
        "ffn2_w_in": w((DEPTH, D_MODEL, 2 * D_FF), D_MODEL),
        "ffn2_w_out": w((DEPTH, D_FF, D_MODEL), D_FF),
        "final_norm": gain((D_MODEL,)),
    }


def reference(x, ffn1_norm, ffn1_w_in, ffn1_w_out, mix_norm, w_qkv, w_gate, sinks,
              lambda_q1, lambda_k1, lambda_q2, lambda_k2, diff_norm,
              w_branch_a, w_branch_b, w_branch_c, w_out,
              ffn2_norm, ffn2_w_in, ffn2_w_out, final_norm):
    b, s, _ = x.shape
    cos_a, sin_a = rope_tables(s, HEAD_DIM)
    cos_c, sin_c = rope_tables(s, DIFF_QK_DIM)
    split_idx = [int(i) for i in np.cumsum(QKV_WIDTHS)[:-1]]

    for l in range(DEPTH):
        x = x + 0.5 * swiglu(rms_norm(x, ffn1_norm[l]), ffn1_w_in[l], ffn1_w_out[l])

        h = rms_norm(x, mix_norm[l])
        qkv = h @ w_qkv[l]
        qa, ka, va, qb, kb, vb, qc, kc, vc = jnp.split(qkv, split_idx, axis=-1)

        qa = apply_rope(qa.reshape(b, s, SWA_Q_HEADS, HEAD_DIM), cos_a, sin_a)
        ka = apply_rope(ka.reshape(b, s, SWA_KV_HEADS, HEAD_DIM), cos_a, sin_a)
        va = va.reshape(b, s, SWA_KV_HEADS, HEAD_DIM)
        out_a = sliding_window_gqa(qa, ka, va, sinks[l])

        out_b = stick_breaking_attention(qb.reshape(b, s, SB_HEADS, HEAD_DIM),
                                         kb.reshape(b, s, SB_HEADS, HEAD_DIM),
                                         vb.reshape(b, s, SB_HEADS, HEAD_DIM))

        qc = apply_rope(qc.reshape(b, s, DIFF_HEADS * 2, DIFF_QK_DIM), cos_c, sin_c)
        kc = apply_rope(kc.reshape(b, s, DIFF_HEADS * 2, DIFF_QK_DIM), cos_c, sin_c)
        qc = qc.reshape(b, s, DIFF_HEADS, 2, DIFF_QK_DIM)
        kc = kc.reshape(b, s, DIFF_HEADS, 2, DIFF_QK_DIM)
        vc = vc.reshape(b, s, DIFF_HEADS, HEAD_DIM)
        lam_init = 0.8 - 0.6 * math.exp(-0.3 * l)
        lam = (jnp.exp(jnp.sum(lambda_q1[l].astype(jnp.float32) * lambda_k1[l].astype(jnp.float32)))
               - jnp.exp(jnp.sum(lambda_q2[l].astype(jnp.float32) * lambda_k2[l].astype(jnp.float32)))
               + lam_init)
        oc = differential_attention(qc, kc, vc, lam)
        oc = rms_norm(oc, diff_norm[l], DIFF_NORM_EPS) * (1.0 - lam_init)
        out_c = oc.reshape(b, s, DIFF_W).astype(x.dtype)

        gates = jax.nn.sigmoid((h @ w_gate[l]).astype(jnp.float32)).astype(x.dtype)
        g_a, g_b, g_c = jnp.split(gates, N_BRANCH, axis=-1)
        merged = (g_a * (out_a @ w_branch_a[l])
                  + g_b * (out_b @ w_branch_b[l])
                  + g_c * (out_c @ w_branch_c[l]))
        x = x + merged @ w_out[l]

        x = x + 0.5 * swiglu(rms_norm(x, ffn2_norm[l]), ffn2_w_in[l], ffn2_w_out[l])

    return rms_norm(x, final_norm)
```

```python
import functools
import math

import jax
import jax.numpy as jnp
from jax import lax
from jax.experimental import pallas as pl
from jax.experimental.pallas import tpu as pltpu

F32 = jnp.float32
BF16 = jnp.bfloat16

HEAD_DIM = 128
ROPE_THETA = 10000.0
EPS = 1e-6
SWA_Q_HEADS = 16
SWA_KV_HEADS = 4
SWA_GROUP = SWA_Q_HEADS // SWA_KV_HEADS
WINDOW = 128
SB_HEADS = 8
DIFF_HEADS = 8
DIFF_QK_DIM = HEAD_DIM // 2
DIFF_NORM_EPS = 1e-5
QA_OFF = 0
KA_OFF = QA_OFF + SWA_Q_HEADS
VA_OFF = KA_OFF + SWA_KV_HEADS
QB_OFF = VA_OFF + SWA_KV_HEADS
KB_OFF = QB_OFF + SB_HEADS
VB_OFF = KB_OFF + SB_HEADS
QC_OFF = VB_OFF + SB_HEADS
KC_OFF = QC_OFF + DIFF_HEADS
VC_OFF = KC_OFF + DIFF_HEADS
QKV_HEADS = VC_OFF + DIFF_HEADS

V7X_VMEM_BYTES = 64 * 1024 * 1024
VMEM_LIMIT_BYTES = V7X_VMEM_BYTES * 3 // 4
LANES = 128
MASK_VALUE = -1e30


def _params(n_grid_dims):
    return pltpu.CompilerParams(
        dimension_semantics=("parallel",) * n_grid_dims,
        vmem_limit_bytes=VMEM_LIMIT_BYTES,
    )


def _rmsnorm_body(x_ref, g_ref, o_ref, *, eps):
    x = x_ref[...]
    ms = jnp.mean(x * x, axis=-1, keepdims=True)
    o_ref[...] = (x * lax.rsqrt(ms + eps) * g_ref[...]).astype(o_ref.dtype)


def _rmsnorm(x, gains, layer, out_dtype, *, block_rows=256):
    m, d = x.shape
    br = min(block_rows, m)
    return pl.pallas_call(
        functools.partial(_rmsnorm_body, eps=EPS),
        grid=(m // br,),
        in_specs=[pl.BlockSpec((br, d), lambda i: (i, 0)),
                  pl.BlockSpec((None, 1, d), lambda i: (layer, 0, 0))],
        out_specs=pl.BlockSpec((br, d), lambda i: (i, 0)),
        out_shape=jax.ShapeDtypeStruct((m, d), out_dtype),
        compiler_params=_params(1),
        name="rmsnorm",
    )(x, gains)


def _swiglu_body(h_ref, wg_ref, wu_ref, o_ref):
    h = h_ref[...]
    g = jnp.dot(h, wg_ref[...], preferred_element_type=F32)
    u = jnp.dot(h, wu_ref[...], preferred_element_type=F32)
    o_ref[...] = (g * jax.nn.sigmoid(g) * u).astype(o_ref.dtype)


def _swiglu_up(h, w_in, layer, *, bm=1024, bn=512):
    m, k = h.shape
    f = w_in.shape[2] // 2
    bm, bn = min(bm, m), min(bn, f)
    nj = f // bn
    return pl.pallas_call(
        _swiglu_body,
        grid=(m // bm, nj),
        in_specs=[pl.BlockSpec((bm, k), lambda i, j: (i, 0)),
                  pl.BlockSpec((None, k, bn), lambda i, j: (layer, 0, j)),
                  pl.BlockSpec((None, k, bn), lambda i, j: (layer, 0, j + nj))],
        out_specs=pl.BlockSpec((bm, bn), lambda i, j: (i, j)),
        out_shape=jax.ShapeDtypeStruct((m, f), BF16),
        compiler_params=_params(2),
        name="swiglu_up",
    )(h, w_in, w_in)


def _residual_body(a_ref, w_ref, r_ref, o_ref, *, scale):
    acc = jnp.dot(a_ref[...], w_ref[...], preferred_element_type=F32)
    o_ref[...] = r_ref[...] + scale * acc


def _residual_proj(a, w, resid, layer, scale, *, bm=1024, bn=512):
    m, k = a.shape
    n = w.shape[2]
    bm, bn = min(bm, m), min(bn, n)
    return pl.pallas_call(
        functools.partial(_residual_body, scale=scale),
        grid=(m // bm, n // bn),
        in_specs=[pl.BlockSpec((bm, k), lambda i, j: (i, 0)),
                  pl.BlockSpec((None, k, bn), lambda i, j: (layer, 0, j)),
                  pl.BlockSpec((bm, bn), lambda i, j: (i, j))],
        out_specs=pl.BlockSpec((bm, bn), lambda i, j: (i, j)),
        out_shape=jax.ShapeDtypeStruct((m, n), F32),
        compiler_params=_params(2),
        name="residual_proj",
    )(a, w, resid)


def _rope_tables(seq):
    def tables(dim):
        inv = 1.0 / (ROPE_THETA ** (jnp.arange(0, dim, 2, dtype=F32) / dim))
        ang = jnp.arange(seq, dtype=F32)[:, None] * inv[None, :]
        cos, sin = jnp.cos(ang), jnp.sin(ang)
        reps = HEAD_DIM // dim
        return (jnp.tile(jnp.concatenate([cos, cos], axis=-1), (1, reps)),
                jnp.tile(jnp.concatenate([-sin, sin], axis=-1), (1, reps)))
    return tables(HEAD_DIM) + tables(DIFF_QK_DIM)


def _qkv_body(h_ref, w_ref, cos_a_ref, sin_a_ref, cos_c_ref, sin_c_ref, o_ref, *, segments, heads_per_block):
    acc = jnp.dot(h_ref[...], w_ref[...], preferred_element_type=F32)
    j = pl.program_id(1)

    def partner_a(x):
        return pltpu.roll(x, HEAD_DIM // 2, axis=1)

    def partner_c(x):
        lane = lax.broadcasted_iota(jnp.int32, x.shape, 1)
        first_half = (lane % DIFF_QK_DIM) < (DIFF_QK_DIM // 2)
        return jnp.where(first_half,
                         pltpu.roll(x, HEAD_DIM - DIFF_QK_DIM // 2, axis=1),
                         pltpu.roll(x, DIFF_QK_DIM // 2, axis=1))

    def emit(rope, scale):
        for s in range(heads_per_block):
            cols = slice(s * HEAD_DIM, (s + 1) * HEAD_DIM)
            x = acc[:, cols]
            if rope == "a":
                x = x * cos_a_ref[...] + partner_a(x) * sin_a_ref[...]
            elif rope == "c":
                x = x * cos_c_ref[...] + partner_c(x) * sin_c_ref[...]
            if scale != 1.0:
                x = x * scale
            o_ref[:, cols] = x.astype(o_ref.dtype)

    for start, stop, rope, scale in segments:
        pl.when((j >= start) & (j < stop))(functools.partial(emit, rope, scale))


def _qkv_proj(h, w_qkv, layer, tables, seq, *, bm=1024, bn=512):
    m, k = h.shape
    n = w_qkv.shape[2]
    bm = min(bm, seq)
    hpb = bn // HEAD_DIM
    scale_a = 1.0 / math.sqrt(HEAD_DIM)
    scale_c = 1.0 / math.sqrt(DIFF_QK_DIM)
    layout = [(QA_OFF, "a", scale_a), (KA_OFF, "a", 1.0), (VA_OFF, None, 1.0),
              (QB_OFF, None, scale_a), (KB_OFF, None, 1.0),
              (QC_OFF, "c", scale_c), (KC_OFF, "c", 1.0), (VC_OFF, None, 1.0)]
    bounds = [off for off, _, _ in layout] + [QKV_HEADS]
    assert all(b % hpb == 0 for b in bounds)
    segments = tuple((bounds[t] // hpb, bounds[t + 1] // hpb, rope, scale)
                     for t, (_, rope, scale) in enumerate(layout))
    pos_blocks = seq // bm
    table_spec = pl.BlockSpec((bm, LANES), lambda i, j: (i % pos_blocks, 0))
    return pl.pallas_call(
        functools.partial(_qkv_body, segments=segments, heads_per_block=hpb),
        grid=(m // bm, n // bn),
        in_specs=[pl.BlockSpec((bm, k), lambda i, j: (i, 0)),
                  pl.BlockSpec((None, k, bn), lambda i, j: (layer, 0, j)),
                  table_spec, table_spec, table_spec, table_spec],
        out_specs=pl.BlockSpec((bm, bn), lambda i, j: (i, j)),
        out_shape=jax.ShapeDtypeStruct((m, n), BF16),
        compiler_params=_params(2),
        name="qkv_proj",
    )(h, w_qkv, *tables)


def _swa_body(sinks_ref, q_ref, kp_ref, kc_ref, vp_ref, vc_ref, o_ref):
    i = pl.program_id(1)
    hk = pl.program_id(2)
    blk = WINDOW
    q = q_ref[...]
    qs = jnp.concatenate([q[:, g * HEAD_DIM:(g + 1) * HEAD_DIM] for g in range(SWA_GROUP)], axis=0)
    k = jnp.concatenate([kp_ref[...], kc_ref[...]], axis=0)
    v = jnp.concatenate([vp_ref[...], vc_ref[...]], axis=0)
    s = lax.dot_general(qs, k, (((1,), (1,)), ((), ())), preferred_element_type=F32)
    rows = SWA_GROUP * blk
    t = lax.broadcasted_iota(jnp.int32, (rows, 2 * blk), 0) % blk
    c = lax.broadcasted_iota(jnp.int32, (rows, 2 * blk), 1)
    valid = (c > t) & (c <= t + blk) & ((c >= blk) | (i > 0))
    s = jnp.where(valid, s, MASK_VALUE)
    group = lax.broadcasted_iota(jnp.int32, (rows, 1), 0) // blk
    sink = jnp.zeros((rows, 1), F32)
    for g in range(SWA_GROUP):
        sink = jnp.where(group == g, sinks_ref[hk * SWA_GROUP + g], sink)
    m = jnp.maximum(jnp.max(s, axis=-1, keepdims=True), sink)
    p = jnp.exp(s - m)
    denom = jnp.sum(p, axis=-1, keepdims=True) + jnp.exp(sink - m)
    o = jnp.dot(p.astype(BF16), v, preferred_element_type=F32) / denom
    for g in range(SWA_GROUP):
        o_ref[:, g * HEAD_DIM:(g + 1) * HEAD_DIM] = o[g * blk:(g + 1) * blk].astype(o_ref.dtype)


def _swa_attention(qkv, sinks, batch, seq):
    m = qkv.shape[0]
    blk = WINDOW
    nq = seq // blk
    gw = SWA_GROUP * HEAD_DIM

    def cur(off):
        return pl.BlockSpec((blk, HEAD_DIM), lambda b, i, hk: (b * nq + i, off + hk))

    def prev(off):
        return pl.BlockSpec((blk, HEAD_DIM), lambda b, i, hk: (b * nq + jnp.maximum(i - 1, 0), off + hk))

    return pl.pallas_call(
        _swa_body,
        grid=(batch, nq, SWA_KV_HEADS),
        in_specs=[pl.BlockSpec(memory_space=pltpu.SMEM),
                  pl.BlockSpec((blk, gw), lambda b, i, hk: (b * nq + i, hk)),
                  prev(KA_OFF), cur(KA_OFF), prev(VA_OFF), cur(VA_OFF)],
        out_specs=pl.BlockSpec((blk, gw), lambda b, i, hk: (b * nq + i, hk)),
        out_shape=jax.ShapeDtypeStruct((m, SWA_Q_HEADS * HEAD_DIM), BF16),
        compiler_params=_params(3),
        name="swa_attention",
    )(sinks, qkv, qkv, qkv, qkv, qkv)


def _stickbreak_body(q_ref, k_ref, v_ref, o_ref, *, blk):
    i = pl.program_id(2)
    q = q_ref[...]
    kj = lax.broadcasted_iota(jnp.int32, (blk, blk), 0)
    ks = lax.broadcasted_iota(jnp.int32, (blk, blk), 1)
    later_keys = jnp.where(kj > ks, 1.0, 0.0).astype(BF16)
    causal = ks < kj

    def tile(kb, run, acc, diagonal):
        start = pl.multiple_of(kb * blk, blk)
        k = k_ref[pl.ds(start, blk), :]
        v = v_ref[pl.ds(start, blk), :]
        z = lax.dot_general(q, k, (((1,), (1,)), ((), ())), preferred_element_type=F32)
        log_beta = jnp.minimum(z, 0.0) - jnp.log1p(jnp.exp(-jnp.abs(z)))
        log_1m = log_beta - z
        if diagonal:
            log_1m = jnp.where(causal, log_1m, 0.0)
        hi = log_1m.astype(BF16)
        lo = (log_1m - hi.astype(F32)).astype(BF16)
        after = (run + jnp.dot(hi, later_keys, preferred_element_type=F32)
                 + jnp.dot(lo, later_keys, preferred_element_type=F32))
        a = jnp.exp(log_beta + after)
        if diagonal:
            a = jnp.where(causal, a, 0.0)
        acc = acc + jnp.dot(a.astype(BF16), v, preferred_element_type=F32)
        run = run + jnp.sum(log_1m, axis=-1, keepdims=True)
        return run, acc

    run0 = jnp.zeros((blk, 1), F32)
    acc0 = jnp.zeros((blk, HEAD_DIM), F32)
    carry = tile(i, run0, acc0, True)
    _, acc = lax.fori_loop(0, i, lambda t, c: tile(i - 1 - t, c[0], c[1], False), carry)
    o_ref[...] = acc.astype(o_ref.dtype)


def _stickbreak_attention(qkv, batch, seq, *, blk=256):
    m = qkv.shape[0]
    blk = min(blk, seq)
    nq = seq // blk
    return pl.pallas_call(
        functools.partial(_stickbreak_body, blk=blk),
        grid=(batch, SB_HEADS, nq),
        in_specs=[pl.BlockSpec((blk, HEAD_DIM), lambda b, h, i: (b * nq + i, QB_OFF + h)),
                  pl.BlockSpec((seq, HEAD_DIM), lambda b, h, i: (b, KB_OFF + h)),
                  pl.BlockSpec((seq, HEAD_DIM), lambda b, h, i: (b, VB_OFF + h))],
        out_specs=pl.BlockSpec((blk, HEAD_DIM), lambda b, h, i: (b * nq + i, h)),
        out_shape=jax.ShapeDtypeStruct((m, SB_HEADS * HEAD_DIM), BF16),
        compiler_params=_params(3),
        name="stickbreak_attention",
    )(qkv, qkv, qkv)


def _diff_body(q_ref, k_ref, v_ref, lam_ref, g_ref, o_ref, *, blk, lam_init):
    i = pl.program_id(2)
    q = q_ref[...]
    lane = lax.broadcasted_iota(jnp.int32, q.shape, 1)
    zero = jnp.zeros_like(q)
    qq = jnp.concatenate([jnp.where(lane < DIFF_QK_DIM, q, zero),
                          jnp.where(lane >= DIFF_QK_DIM, q, zero)], axis=0)
    t = lax.broadcasted_iota(jnp.int32, (2 * blk, blk), 0) % blk
    c = lax.broadcasted_iota(jnp.int32, (2 * blk, blk), 1)
    causal = c <= t

    def tile(kb, m, l, acc, diagonal):
        start = pl.multiple_of(kb * blk, blk)
        k = k_ref[pl.ds(start, blk), :]
        v = v_ref[pl.ds(start, blk), :]
        s = lax.dot_general(qq, k, (((1,), (1,)), ((), ())), preferred_element_type=F32)
        if diagonal:
            s = jnp.where(causal, s, MASK_VALUE)
        m_new = jnp.maximum(m, jnp.max(s, axis=-1, keepdims=True))
        alpha = jnp.exp(m - m_new)
        p = jnp.exp(s - m_new)
        l = alpha * l + jnp.sum(p, axis=-1, keepdims=True)
        acc = alpha * acc + jnp.dot(p.astype(BF16), v, preferred_element_type=F32)
        return m_new, l, acc

    m0 = jnp.full((2 * blk, 1), MASK_VALUE, F32)
    l0 = jnp.zeros((2 * blk, 1), F32)
    acc0 = jnp.zeros((2 * blk, HEAD_DIM), F32)
    carry = tile(i, m0, l0, acc0, True)
    _, l, acc = lax.fori_loop(0, i, lambda kb, c: tile(kb, c[0], c[1], c[2], False), carry)
    o = acc / l
    lam_vecs = lam_ref[...]
    lam = (jnp.exp(jnp.sum(lam_vecs[0:1] * lam_vecs[1:2], axis=-1, keepdims=True))
           - jnp.exp(jnp.sum(lam_vecs[2:3] * lam_vecs[3:4], axis=-1, keepdims=True))
           + lam_init)
    d = o[:blk] - lam * o[blk:]
    ms = jnp.mean(d * d, axis=-1, keepdims=True)
    y = d * lax.rsqrt(ms + DIFF_NORM_EPS) * g_ref[...]
    o_ref[...] = (y * (1.0 - lam_init)).astype(o_ref.dtype)


def _diff_attention(qkv, lam_vecs, diff_norm, layer, lam_init, batch, seq, *, blk=256):
    m = qkv.shape[0]
    blk = min(blk, seq)
    nq = seq // blk
    return pl.pallas_call(
        functools.partial(_diff_body, blk=blk, lam_init=lam_init),
        grid=(batch, DIFF_HEADS, nq),
        in_specs=[pl.BlockSpec((blk, HEAD_DIM), lambda b, h, i: (b * nq + i, QC_OFF + h)),
                  pl.BlockSpec((seq, HEAD_DIM), lambda b, h, i: (b, KC_OFF + h)),
                  pl.BlockSpec((seq, HEAD_DIM), lambda b, h, i: (b, VC_OFF + h)),
                  pl.BlockSpec(lam_vecs.shape, lambda b, h, i: (0, 0)),
                  pl.BlockSpec((None, 1, HEAD_DIM), lambda b, h, i: (layer, 0, 0))],
        out_specs=pl.BlockSpec((blk, HEAD_DIM), lambda b, h, i: (b * nq + i, h)),
        out_shape=jax.ShapeDtypeStruct((m, DIFF_HEADS * HEAD_DIM), BF16),
        compiler_params=_params(3),
        name="diff_attention",
    )(qkv, qkv, qkv, lam_vecs, diff_norm)


def _merge_body(h_ref, wga_ref, wgb_ref, wgc_ref, oa_ref, wa_ref, ob_ref, wb_ref, oc_ref, wc_ref, o_ref):
    h = h_ref[...]

    def gated(wg_ref, x_ref, w_ref):
        gate = jax.nn.sigmoid(jnp.dot(h, wg_ref[...], preferred_element_type=F32))
        return gate * jnp.dot(x_ref[...], w_ref[...], preferred_element_type=F32)

    merged = gated(wga_ref, oa_ref, wa_ref) + gated(wgb_ref, ob_ref, wb_ref) + gated(wgc_ref, oc_ref, wc_ref)
    o_ref[...] = merged.astype(o_ref.dtype)


def _gated_merge(h, w_gate, out_a, w_a, out_b, w_b, out_c, w_c, layer, *, bm=512, bn=256):
    m, k = h.shape
    d = w_a.shape[2]
    bm, bn = min(bm, m), min(bn, d)
    nj = d // bn

    def act(x):
        return pl.BlockSpec((bm, x.shape[1]), lambda i, j: (i, 0))

    def weight(w, block_off=0):
        return pl.BlockSpec((None, w.shape[1], bn), lambda i, j: (layer, 0, j + block_off))

    return pl.pallas_call(
        _merge_body,
        grid=(m // bm, nj),
        in_specs=[act(h), weight(w_gate), weight(w_gate, nj), weight(w_gate, 2 * nj),
                  act(out_a), weight(w_a), act(out_b), weight(w_b), act(out_c), weight(w_c)],
        out_specs=pl.BlockSpec((bm, bn), lambda i, j: (i, j)),
        out_shape=jax.ShapeDtypeStruct((m, d), BF16),
        compiler_params=_params(2),
        name="gated_merge",
    )(h, w_gate, w_gate, w_gate, out_a, w_a, out_b, w_b, out_c, w_c)


def kernel(x, ffn1_norm, ffn1_w_in, ffn1_w_out, mix_norm, w_qkv, w_gate, sinks, lambda_q1, lambda_k1, lambda_q2, lambda_k2, diff_norm, w_branch_a, w_branch_b, w_branch_c, w_out, ffn2_norm, ffn2_w_in, ffn2_w_out, final_norm):
    batch, seq, d_model = x.shape
    depth = ffn1_w_in.shape[0]
    assert w_qkv.shape[2] == QKV_HEADS * HEAD_DIM
    m = batch * seq

    def weights(w):
        return w.astype(BF16)

    def gains(g):
        return g.astype(F32).reshape(g.shape[0], 1, g.shape[1])

    ffn1_w_in, ffn1_w_out, ffn2_w_in, ffn2_w_out = map(weights, (ffn1_w_in, ffn1_w_out, ffn2_w_in, ffn2_w_out))
    w_qkv, w_gate, w_out = map(weights, (w_qkv, w_gate, w_out))
    w_branch_a, w_branch_b, w_branch_c = map(weights, (w_branch_a, w_branch_b, w_branch_c))
    ffn1_norm, mix_norm, ffn2_norm, diff_norm = map(gains, (ffn1_norm, mix_norm, ffn2_norm, diff_norm))
    final_gain = final_norm.astype(F32).reshape(1, 1, d_model)
    tables = _rope_tables(seq)
    sinks = sinks.astype(F32)
    lam_all = jnp.stack([lambda_q1, lambda_k1, lambda_q2, lambda_k2], axis=1).astype(F32)

    x = x.reshape(m, d_model)
    for layer in range(depth):
        h = _rmsnorm(x, ffn1_norm, layer, BF16)
        x = _residual_proj(_swiglu_up(h, ffn1_w_in, layer), ffn1_w_out, x, layer, 0.5)

        h = _rmsnorm(x, mix_norm, layer, BF16)
        qkv = _qkv_proj(h, w_qkv, layer, tables, seq)
        out_a = _swa_attention(qkv, sinks[layer], batch, seq)
        out_b = _stickbreak_attention(qkv, batch, seq)
        lam_init = 0.8 - 0.6 * math.exp(-0.3 * layer)
        out_c = _diff_attention(qkv, lam_all[layer], diff_norm, layer, lam_init, batch, seq)
        merged = _gated_merge(h, w_gate, out_a, w_branch_a, out_b, w_branch_b, out_c, w_branch_c, layer)
        x = _residual_proj(merged, w_out, x, layer, 1.0)

        h = _rmsnorm(x, ffn2_norm, layer, BF16)
        x = _residual_proj(_swiglu_up(h, ffn2_w_in, layer), ffn2_w_out, x, layer, 0.5)

    out = _rmsnorm(x, final_gain, 0, F32)
    return out.reshape(batch, seq, d_model)
```

```python
import functools
import math

import jax
import jax.numpy as jnp
from jax import lax
from jax.experimental import pallas as pl
from jax.experimental.pallas import tpu as pltpu

F32 = jnp.float32
BF16 = jnp.bfloat16

HEAD_DIM = 128
ROPE_THETA = 10000.0
EPS = 1e-6
SWA_Q_HEADS = 16
SWA_KV_HEADS = 4
SWA_GROUP = SWA_Q_HEADS // SWA_KV_HEADS
WINDOW = 128
SB_HEADS = 8
DIFF_HEADS = 8
DIFF_QK_DIM = HEAD_DIM // 2
DIFF_NORM_EPS = 1e-5
QA_OFF = 0
KA_OFF = QA_OFF + SWA_Q_HEADS
VA_OFF = KA_OFF + SWA_KV_HEADS
QB_OFF = VA_OFF + SWA_KV_HEADS
KB_OFF = QB_OFF + SB_HEADS
VB_OFF = KB_OFF + SB_HEADS
QC_OFF = VB_OFF + SB_HEADS
KC_OFF = QC_OFF + DIFF_HEADS
VC_OFF = KC_OFF + DIFF_HEADS
QKV_HEADS = VC_OFF + DIFF_HEADS

V7X_VMEM_BYTES = 64 * 1024 * 1024
VMEM_LIMIT_BYTES = V7X_VMEM_BYTES * 3 // 4
LANES = 128
MASK_VALUE = -1e30
LOG2_E = math.log2(math.e)


def _params(n_grid_dims):
    return pltpu.CompilerParams(
        dimension_semantics=("parallel",) * n_grid_dims,
        vmem_limit_bytes=VMEM_LIMIT_BYTES,
    )


def _rmsnorm_body(x_ref, g_ref, o_ref, *, eps):
    x = x_ref[...]
    ms = jnp.mean(x * x, axis=-1, keepdims=True)
    o_ref[...] = (x * lax.rsqrt(ms + eps) * g_ref[...]).astype(o_ref.dtype)


def _rmsnorm(x, gains, layer, out_dtype, *, block_rows=256):
    m, d = x.shape
    br = min(block_rows, m)
    return pl.pallas_call(
        functools.partial(_rmsnorm_body, eps=EPS),
        grid=(m // br,),
        in_specs=[pl.BlockSpec((br, d), lambda i: (i, 0)),
                  pl.BlockSpec((None, 1, d), lambda i: (layer, 0, 0))],
        out_specs=pl.BlockSpec((br, d), lambda i: (i, 0)),
        out_shape=jax.ShapeDtypeStruct((m, d), out_dtype),
        compiler_params=_params(1),
        name="rmsnorm",
    )(x, gains)


def _swiglu_body(h_ref, wg_ref, wu_ref, o_ref):
    h = h_ref[...]
    g = jnp.dot(h, wg_ref[...], preferred_element_type=F32)
    u = jnp.dot(h, wu_ref[...], preferred_element_type=F32)
    o_ref[...] = (g * jax.nn.sigmoid(g) * u).astype(o_ref.dtype)


def _swiglu_up(h, w_in, layer, *, bm=1024, bn=512):
    m, k = h.shape
    f = w_in.shape[2] // 2
    bm, bn = min(bm, m), min(bn, f)
    nj = f // bn
    return pl.pallas_call(
        _swiglu_body,
        grid=(m // bm, nj),
        in_specs=[pl.BlockSpec((bm, k), lambda i, j: (i, 0)),
                  pl.BlockSpec((None, k, bn), lambda i, j: (layer, 0, j)),
                  pl.BlockSpec((None, k, bn), lambda i, j: (layer, 0, j + nj))],
        out_specs=pl.BlockSpec((bm, bn), lambda i, j: (i, j)),
        out_shape=jax.ShapeDtypeStruct((m, f), BF16),
        compiler_params=_params(2),
        name="swiglu_up",
    )(h, w_in, w_in)


def _residual_body(a_ref, w_ref, r_ref, o_ref, *, scale):
    acc = jnp.dot(a_ref[...], w_ref[...], preferred_element_type=F32)
    o_ref[...] = r_ref[...] + scale * acc


def _residual_proj(a, w, resid, layer, scale, *, bm=1024, bn=512):
    m, k = a.shape
    n = w.shape[2]
    bm, bn = min(bm, m), min(bn, n)
    return pl.pallas_call(
        functools.partial(_residual_body, scale=scale),
        grid=(m // bm, n // bn),
        in_specs=[pl.BlockSpec((bm, k), lambda i, j: (i, 0)),
                  pl.BlockSpec((None, k, bn), lambda i, j: (layer, 0, j)),
                  pl.BlockSpec((bm, bn), lambda i, j: (i, j))],
        out_specs=pl.BlockSpec((bm, bn), lambda i, j: (i, j)),
        out_shape=jax.ShapeDtypeStruct((m, n), F32),
        compiler_params=_params(2),
        name="residual_proj",
    )(a, w, resid)


def _rope_tables(seq):
    def tables(dim):
        inv = 1.0 / (ROPE_THETA ** (jnp.arange(0, dim, 2, dtype=F32) / dim))
        ang = jnp.arange(seq, dtype=F32)[:, None] * inv[None, :]
        cos, sin = jnp.cos(ang), jnp.sin(ang)
        reps = HEAD_DIM // dim
        return (jnp.tile(jnp.concatenate([cos, cos], axis=-1), (1, reps)),
                jnp.tile(jnp.concatenate([-sin, sin], axis=-1), (1, reps)))
    return tables(HEAD_DIM) + tables(DIFF_QK_DIM)


def _qkv_body(h_ref, w_ref, cos_a_ref, sin_a_ref, cos_c_ref, sin_c_ref, o_ref, *, segments, heads_per_block):
    acc = jnp.dot(h_ref[...], w_ref[...], preferred_element_type=F32)
    j = pl.program_id(1)

    def partner_a(x):
        return pltpu.roll(x, HEAD_DIM // 2, axis=1)

    def partner_c(x):
        lane = lax.broadcasted_iota(jnp.int32, x.shape, 1)
        first_half = (lane % DIFF_QK_DIM) < (DIFF_QK_DIM // 2)
        return jnp.where(first_half,
                         pltpu.roll(x, HEAD_DIM - DIFF_QK_DIM // 2, axis=1),
                         pltpu.roll(x, DIFF_QK_DIM // 2, axis=1))

    def emit(rope, scale):
        for s in range(heads_per_block):
            cols = slice(s * HEAD_DIM, (s + 1) * HEAD_DIM)
            x = acc[:, cols]
            if rope == "a":
                x = x * cos_a_ref[...] + partner_a(x) * sin_a_ref[...]
            elif rope == "c":
                x = x * cos_c_ref[...] + partner_c(x) * sin_c_ref[...]
            if scale != 1.0:
                x = x * scale
            o_ref[:, cols] = x.astype(o_ref.dtype)

    for start, stop, rope, scale in segments:
        pl.when((j >= start) & (j < stop))(functools.partial(emit, rope, scale))


def _qkv_proj(h, w_qkv, layer, tables, seq, *, bm=1024, bn=512):
    m, k = h.shape
    n = w_qkv.shape[2]
    bm = min(bm, seq)
    hpb = bn // HEAD_DIM
    scale_a = LOG2_E / math.sqrt(HEAD_DIM)
    scale_c = LOG2_E / math.sqrt(DIFF_QK_DIM)
    layout = [(QA_OFF, "a", scale_a), (KA_OFF, "a", 1.0), (VA_OFF, None, 1.0),
              (QB_OFF, None, scale_a), (KB_OFF, None, 1.0),
              (QC_OFF, "c", scale_c), (KC_OFF, "c", 1.0), (VC_OFF, None, 1.0)]
    bounds = [off for off, _, _ in layout] + [QKV_HEADS]
    assert all(b % hpb == 0 for b in bounds)
    segments = tuple((bounds[t] // hpb, bounds[t + 1] // hpb, rope, scale)
                     for t, (_, rope, scale) in enumerate(layout))
    pos_blocks = seq // bm
    table_spec = pl.BlockSpec((bm, LANES), lambda i, j: (i % pos_blocks, 0))
    return pl.pallas_call(
        functools.partial(_qkv_body, segments=segments, heads_per_block=hpb),
        grid=(m // bm, n // bn),
        in_specs=[pl.BlockSpec((bm, k), lambda i, j: (i, 0)),
                  pl.BlockSpec((None, k, bn), lambda i, j: (layer, 0, j)),
                  table_spec, table_spec, table_spec, table_spec],
        out_specs=pl.BlockSpec((bm, bn), lambda i, j: (i, j)),
        out_shape=jax.ShapeDtypeStruct((m, n), BF16),
        compiler_params=_params(2),
        name="qkv_proj",
    )(h, w_qkv, *tables)


def _swa_body(sinks_ref, q_ref, kp_ref, kc_ref, vp_ref, vc_ref, o_ref):
    i = pl.program_id(1)
    hk = pl.program_id(2)
    blk = WINDOW
    q = q_ref[...]
    qs = jnp.concatenate([q[:, g * HEAD_DIM:(g + 1) * HEAD_DIM] for g in range(SWA_GROUP)], axis=0)
    k = jnp.concatenate([kp_ref[...], kc_ref[...]], axis=0)
    v = jnp.concatenate([vp_ref[...], vc_ref[...]], axis=0)
    s = lax.dot_general(qs, k, (((1,), (1,)), ((), ())), preferred_element_type=F32)
    rows = SWA_GROUP * blk
    t = lax.broadcasted_iota(jnp.int32, (rows, 2 * blk), 0) % blk
    c = lax.broadcasted_iota(jnp.int32, (rows, 2 * blk), 1)
    valid = (c > t) & (c <= t + blk) & ((c >= blk) | (i > 0))
    s = jnp.where(valid, s, MASK_VALUE)
    group = lax.broadcasted_iota(jnp.int32, (rows, 1), 0) // blk
    sink = jnp.zeros((rows, 1), F32)
    for g in range(SWA_GROUP):
        sink = jnp.where(group == g, sinks_ref[hk * SWA_GROUP + g] * LOG2_E, sink)
    m = jnp.maximum(jnp.max(s, axis=-1, keepdims=True), sink)
    p = jnp.exp2(s - m)
    denom = jnp.sum(p, axis=-1, keepdims=True) + jnp.exp2(sink - m)
    o = jnp.dot(p.astype(BF16), v, preferred_element_type=F32) / denom
    for g in range(SWA_GROUP):
        o_ref[:, g * HEAD_DIM:(g + 1) * HEAD_DIM] = o[g * blk:(g + 1) * blk].astype(o_ref.dtype)


def _swa_attention(qkv, sinks, batch, seq):
    m = qkv.shape[0]
    blk = WINDOW
    nq = seq // blk
    gw = SWA_GROUP * HEAD_DIM

    def cur(off):
        return pl.BlockSpec((blk, HEAD_DIM), lambda b, i, hk: (b * nq + i, off + hk))

    def prev(off):
        return pl.BlockSpec((blk, HEAD_DIM), lambda b, i, hk: (b * nq + jnp.maximum(i - 1, 0), off + hk))

    return pl.pallas_call(
        _swa_body,
        grid=(batch, nq, SWA_KV_HEADS),
        in_specs=[pl.BlockSpec(memory_space=pltpu.SMEM),
                  pl.BlockSpec((blk, gw), lambda b, i, hk: (b * nq + i, hk)),
                  prev(KA_OFF), cur(KA_OFF), prev(VA_OFF), cur(VA_OFF)],
        out_specs=pl.BlockSpec((blk, gw), lambda b, i, hk: (b * nq + i, hk)),
        out_shape=jax.ShapeDtypeStruct((m, SWA_Q_HEADS * HEAD_DIM), BF16),
        compiler_params=_params(3),
        name="swa_attention",
    )(sinks, qkv, qkv, qkv, qkv, qkv)


def _stickbreak_body(q_ref, k_ref, v_ref, o_ref, *, blk, heads, chunk):
    i = pl.program_id(2)
    kj = lax.broadcasted_iota(jnp.int32, (blk, blk), 0)
    ks = lax.broadcasted_iota(jnp.int32, (blk, blk), 1)
    later_keys = jnp.where(kj > ks, 1.0, 0.0).astype(BF16)
    causal = ks < kj
    row_chunks = [slice(r, r + chunk) for r in range(0, blk, chunk)]

    def tile(kb, state, diagonal):
        rows = pl.ds(pl.multiple_of(kb * blk, blk), blk)
        head_cols = [slice(c * HEAD_DIM, (c + 1) * HEAD_DIM) for c in range(heads)]
        z = [lax.dot_general(q_ref[:, cols], k_ref[rows, cols], (((1,), (1,)), ((), ())),
                             preferred_element_type=F32) for cols in head_cols]
        log_beta, later, totals = [], [], []
        for c in range(heads):
            lb, hi, lo, tot = [], [], [], []
            for rc in row_chunks:
                zc = z[c][rc]
                softplus = jnp.maximum(zc, 0.0) + jnp.log(1.0 + jnp.exp2(-jnp.abs(zc))) * LOG2_E
                lb.append(zc - softplus)
                if diagonal:
                    softplus = jnp.where(causal[rc], softplus, 0.0)
                hi.append(softplus.astype(BF16))
                lo.append((softplus - hi[-1].astype(F32)).astype(BF16))
                tot.append(jnp.sum(softplus, axis=-1, keepdims=True))
            log_beta.append(lb)
            later.append(jnp.dot(jnp.concatenate(hi, axis=0), later_keys, preferred_element_type=F32)
                         + jnp.dot(jnp.concatenate(lo, axis=0), later_keys, preferred_element_type=F32))
            totals.append(jnp.concatenate(tot, axis=0))
        new_state = []
        for c in range(heads):
            run, acc = state[c]
            a = []
            for n, rc in enumerate(row_chunks):
                ac = jnp.exp2(log_beta[c][n] - (run[rc] + later[c][rc]))
                if diagonal:
                    ac = jnp.where(causal[rc], ac, 0.0)
                a.append(ac.astype(BF16))
            acc = acc + jnp.dot(jnp.concatenate(a, axis=0), v_ref[rows, head_cols[c]],
                                preferred_element_type=F32)
            new_state.append((run + totals[c], acc))
        return tuple(new_state)

    state = tuple((jnp.zeros((blk, 1), F32), jnp.zeros((blk, HEAD_DIM), F32)) for _ in range(heads))
    state = tile(i, state, True)
    state = lax.fori_loop(0, i, lambda t, s: tile(i - 1 - t, s, False), state)
    for c in range(heads):
        o_ref[:, c * HEAD_DIM:(c + 1) * HEAD_DIM] = state[c][1].astype(o_ref.dtype)


def _stickbreak_attention(qkv, batch, seq, *, blk=256, heads=4, chunk=64):
    m = qkv.shape[0]
    blk = min(blk, seq)
    nq = seq // blk
    width = heads * HEAD_DIM
    assert all(off % heads == 0 for off in (QB_OFF, KB_OFF, VB_OFF, SB_HEADS))
    return pl.pallas_call(
        functools.partial(_stickbreak_body, blk=blk, heads=heads, chunk=min(chunk, blk)),
        grid=(batch, SB_HEADS // heads, nq),
        in_specs=[pl.BlockSpec((blk, width), lambda b, h, i: (b * nq + i, QB_OFF // heads + h)),
                  pl.BlockSpec((seq, width), lambda b, h, i: (b, KB_OFF // heads + h)),
                  pl.BlockSpec((seq, width), lambda b, h, i: (b, VB_OFF // heads + h))],
        out_specs=pl.BlockSpec((blk, width), lambda b, h, i: (b * nq + i, h)),
        out_shape=jax.ShapeDtypeStruct((m, SB_HEADS * HEAD_DIM), BF16),
        compiler_params=_params(3),
        name="stickbreak_attention",
    )(qkv, qkv, qkv)


def _diff_body(q_ref, k_ref, v_ref, lam_ref, g_ref, o_ref, *, blk, heads, lam_init):
    i = pl.program_id(2)
    lane = lax.broadcasted_iota(jnp.int32, (blk, HEAD_DIM), 1)
    t = lax.broadcasted_iota(jnp.int32, (2 * blk, blk), 0) % blk
    c = lax.broadcasted_iota(jnp.int32, (2 * blk, blk), 1)
    causal = c <= t

    def both_maps(q):
        zero = jnp.zeros_like(q)
        return jnp.concatenate([jnp.where(lane < DIFF_QK_DIM, q, zero),
                                jnp.where(lane >= DIFF_QK_DIM, q, zero)], axis=0)

    qq = [both_maps(q_ref[:, h * HEAD_DIM:(h + 1) * HEAD_DIM]) for h in range(heads)]

    def tile(kb, state, diagonal):
        rows = pl.ds(pl.multiple_of(kb * blk, blk), blk)
        head_cols = [slice(h * HEAD_DIM, (h + 1) * HEAD_DIM) for h in range(heads)]
        scores = [lax.dot_general(qq[h], k_ref[rows, head_cols[h]], (((1,), (1,)), ((), ())),
                                  preferred_element_type=F32) for h in range(heads)]
        new_state = []
        for h in range(heads):
            m, l, acc = state[h]
            s = jnp.where(causal, scores[h], MASK_VALUE) if diagonal else scores[h]
            m_new = jnp.maximum(m, jnp.max(s, axis=-1, keepdims=True))
            alpha = jnp.exp2(m - m_new)
            p = jnp.exp2(s - m_new)
            l = alpha * l + jnp.sum(p, axis=-1, keepdims=True)
            acc = alpha * acc + jnp.dot(p.astype(BF16), v_ref[rows, head_cols[h]], preferred_element_type=F32)
            new_state.append((m_new, l, acc))
        return tuple(new_state)

    state = tuple((jnp.full((2 * blk, 1), MASK_VALUE, F32),
                   jnp.zeros((2 * blk, 1), F32),
                   jnp.zeros((2 * blk, HEAD_DIM), F32)) for _ in range(heads))
    state = tile(i, state, True)
    state = lax.fori_loop(0, i, lambda kb, s: tile(kb, s, False), state)

    lam_vecs = lam_ref[...]
    lam = (jnp.exp(jnp.sum(lam_vecs[0:1] * lam_vecs[1:2], axis=-1, keepdims=True))
           - jnp.exp(jnp.sum(lam_vecs[2:3] * lam_vecs[3:4], axis=-1, keepdims=True))
           + lam_init)
    for h in range(heads):
        _, l, acc = state[h]
        o = acc / l
        d = o[:blk] - lam * o[blk:]
        ms = jnp.mean(d * d, axis=-1, keepdims=True)
        y = d * lax.rsqrt(ms + DIFF_NORM_EPS) * g_ref[...]
        o_ref[:, h * HEAD_DIM:(h + 1) * HEAD_DIM] = (y * (1.0 - lam_init)).astype(o_ref.dtype)


def _diff_attention(qkv, lam_vecs, diff_norm, layer, lam_init, batch, seq, *, blk=256, heads=2):
    m = qkv.shape[0]
    blk = min(blk, seq)
    nq = seq // blk
    width = heads * HEAD_DIM
    assert all(off % heads == 0 for off in (QC_OFF, KC_OFF, VC_OFF, DIFF_HEADS))
    return pl.pallas_call(
        functools.partial(_diff_body, blk=blk, heads=heads, lam_init=lam_init),
        grid=(batch, DIFF_HEADS // heads, nq),
        in_specs=[pl.BlockSpec((blk, width), lambda b, h, i: (b * nq + i, QC_OFF // heads + h)),
                  pl.BlockSpec((seq, width), lambda b, h, i: (b, KC_OFF // heads + h)),
                  pl.BlockSpec((seq, width), lambda b, h, i: (b, VC_OFF // heads + h)),
                  pl.BlockSpec(lam_vecs.shape, lambda b, h, i: (0, 0)),
                  pl.BlockSpec((None, 1, HEAD_DIM), lambda b, h, i: (layer, 0, 0))],
        out_specs=pl.BlockSpec((blk, width), lambda b, h, i: (b * nq + i, h)),
        out_shape=jax.ShapeDtypeStruct((m, DIFF_HEADS * HEAD_DIM), BF16),
        compiler_params=_params(3),
        name="diff_attention",
    )(qkv, qkv, qkv, lam_vecs, diff_norm)


def _merge_body(h_ref, wga_ref, wgb_ref, wgc_ref, oa_ref, wa_ref, ob_ref, wb_ref, oc_ref, wc_ref, o_ref):
    h = h_ref[...]

    def gated(wg_ref, x_ref, w_ref):
        gate = jax.nn.sigmoid(jnp.dot(h, wg_ref[...], preferred_element_type=F32))
        return gate * jnp.dot(x_ref[...], w_ref[...], preferred_element_type=F32)

    merged = gated(wga_ref, oa_ref, wa_ref) + gated(wgb_ref, ob_ref, wb_ref) + gated(wgc_ref, oc_ref, wc_ref)
    o_ref[...] = merged.astype(o_ref.dtype)


def _gated_merge(h, w_gate, out_a, w_a, out_b, w_b, out_c, w_c, layer, *, bm=512, bn=256):
    m, k = h.shape
    d = w_a.shape[2]
    bm, bn = min(bm, m), min(bn, d)
    nj = d // bn

    def act(x):
        return pl.BlockSpec((bm, x.shape[1]), lambda i, j: (i, 0))

    def weight(w, block_off=0):
        return pl.BlockSpec((None, w.shape[1], bn), lambda i, j: (layer, 0, j + block_off))

    return pl.pallas_call(
        _merge_body,
        grid=(m // bm, nj),
        in_specs=[act(h), weight(w_gate), weight(w_gate, nj), weight(w_gate, 2 * nj),
                  act(out_a), weight(w_a), act(out_b), weight(w_b), act(out_c), weight(w_c)],
        out_specs=pl.BlockSpec((bm, bn), lambda i, j: (i, j)),
        out_shape=jax.ShapeDtypeStruct((m, d), BF16),
        compiler_params=_params(2),
        name="gated_merge",
    )(h, w_gate, w_gate, w_gate, out_a, w_a, out_b, w_b, out_c, w_c)


def kernel(x, ffn1_norm, ffn1_w_in, ffn1_w_out, mix_norm, w_qkv, w_gate, sinks, lambda_q1, lambda_k1, lambda_q2, lambda_k2, diff_norm, w_branch_a, w_branch_b, w_branch_c, w_out, ffn2_norm, ffn2_w_in, ffn2_w_out, final_norm):
    batch, seq, d_model = x.shape
    depth = ffn1_w_in.shape[0]
    assert w_qkv.shape[2] == QKV_HEADS * HEAD_DIM
    m = batch * seq

    def weights(w):
        return w.astype(BF16)

    def gains(g):
        return g.astype(F32).reshape(g.shape[0], 1, g.shape[1])

    ffn1_w_in, ffn1_w_out, ffn2_w_in, ffn2_w_out = map(weights, (ffn1_w_in, ffn1_w_out, ffn2_w_in, ffn2_w_out))
    w_qkv, w_gate, w_out = map(weights, (w_qkv, w_gate, w_out))
    w_branch_a, w_branch_b, w_branch_c = map(weights, (w_branch_a, w_branch_b, w_branch_c))
    ffn1_norm, mix_norm, ffn2_norm, diff_norm = map(gains, (ffn1_norm, mix_norm, ffn2_norm, diff_norm))
    final_gain = final_norm.astype(F32).reshape(1, 1, d_model)
    tables = _rope_tables(seq)
    sinks = sinks.astype(F32)
    lam_all = jnp.stack([lambda_q1, lambda_k1, lambda_q2, lambda_k2], axis=1).astype(F32)

    x = x.reshape(m, d_model)
    for layer in range(depth):
        h = _rmsnorm(x, ffn1_norm, layer, BF16)
        x = _residual_proj(_swiglu_up(h, ffn1_w_in, layer), ffn1_w_out, x, layer, 0.5)

        h = _rmsnorm(x, mix_norm, layer, BF16)
        qkv = _qkv_proj(h, w_qkv, layer, tables, seq)
        out_a = _swa_attention(qkv, sinks[layer], batch, seq)
        out_b = _stickbreak_attention(qkv, batch, seq)
        lam_init = 0.8 - 0.6 * math.exp(-0.3 * layer)
        out_c = _diff_attention(qkv, lam_all[layer], diff_norm, layer, lam_init, batch, seq)
        merged = _gated_merge(h, w_gate, out_a, w_branch_a, out_b, w_branch_b, out_c, w_branch_c, layer)
        x = _residual_proj(merged, w_out, x, layer, 1.0)

        h = _rmsnorm(x, ffn2_norm, layer, BF16)
        x = _residual_proj(_swiglu_up(h, ffn2_w_in, layer), ffn2_w_out, x, layer, 0.5)

    out = _rmsnorm(x, final_gain, 0, F32)
    return out.reshape(batch, seq, d_model)
```

```python
import functools
import math

import jax
import jax.numpy as jnp
from jax import lax
from jax.experimental import pallas as pl
from jax.experimental.pallas import tpu as pltpu

F32 = jnp.float32
BF16 = jnp.bfloat16

HEAD_DIM = 128
ROPE_THETA = 10000.0
EPS = 1e-6
SWA_Q_HEADS = 16
SWA_KV_HEADS = 4
SWA_GROUP = SWA_Q_HEADS // SWA_KV_HEADS
WINDOW = 128
SB_HEADS = 8
DIFF_HEADS = 8
DIFF_QK_DIM = HEAD_DIM // 2
DIFF_NORM_EPS = 1e-5
QA_OFF = 0
KA_OFF = QA_OFF + SWA_Q_HEADS
VA_OFF = KA_OFF + SWA_KV_HEADS
QB_OFF = VA_OFF + SWA_KV_HEADS
KB_OFF = QB_OFF + SB_HEADS
VB_OFF = KB_OFF + SB_HEADS
QC_OFF = VB_OFF + SB_HEADS
KC_OFF = QC_OFF + DIFF_HEADS
VC_OFF = KC_OFF + DIFF_HEADS
QKV_HEADS = VC_OFF + DIFF_HEADS

V7X_VMEM_BYTES = 64 * 1024 * 1024
VMEM_LIMIT_BYTES = V7X_VMEM_BYTES * 3 // 4
LANES = 128
MASK_VALUE = -1e30
LOG2_E = math.log2(math.e)


def _params(n_grid_dims, sequential_last=False):
    semantics = ("parallel",) * n_grid_dims
    if sequential_last:
        semantics = semantics[:-1] + ("arbitrary",)
    return pltpu.CompilerParams(dimension_semantics=semantics, vmem_limit_bytes=VMEM_LIMIT_BYTES)


def _rmsnorm_body(x_ref, g_ref, o_ref, *, eps):
    x = x_ref[...]
    ms = jnp.mean(x * x, axis=-1, keepdims=True)
    o_ref[...] = (x * lax.rsqrt(ms + eps) * g_ref[...]).astype(o_ref.dtype)


def _rmsnorm(x, gains, layer, out_dtype, *, block_rows=256):
    m, d = x.shape
    br = min(block_rows, m)
    return pl.pallas_call(
        functools.partial(_rmsnorm_body, eps=EPS),
        grid=(m // br,),
        in_specs=[pl.BlockSpec((br, d), lambda i: (i, 0)),
                  pl.BlockSpec((None, 1, d), lambda i: (layer, 0, 0))],
        out_specs=pl.BlockSpec((br, d), lambda i: (i, 0)),
        out_shape=jax.ShapeDtypeStruct((m, d), out_dtype),
        compiler_params=_params(1),
        name="rmsnorm",
    )(x, gains)


def _swiglu_body(h_ref, wg_ref, wu_ref, o_ref):
    h = h_ref[...]
    g = jnp.dot(h, wg_ref[...], preferred_element_type=F32)
    u = jnp.dot(h, wu_ref[...], preferred_element_type=F32)
    o_ref[...] = (g * jax.nn.sigmoid(g) * u).astype(o_ref.dtype)


def _swiglu_up(h, w_in, layer, *, bm=1024, bn=512):
    m, k = h.shape
    f = w_in.shape[2] // 2
    bm, bn = min(bm, m), min(bn, f)
    nj = f // bn
    return pl.pallas_call(
        _swiglu_body,
        grid=(m // bm, nj),
        in_specs=[pl.BlockSpec((bm, k), lambda i, j: (i, 0)),
                  pl.BlockSpec((None, k, bn), lambda i, j: (layer, 0, j)),
                  pl.BlockSpec((None, k, bn), lambda i, j: (layer, 0, j + nj))],
        out_specs=pl.BlockSpec((bm, bn), lambda i, j: (i, j)),
        out_shape=jax.ShapeDtypeStruct((m, f), BF16),
        compiler_params=_params(2),
        name="swiglu_up",
    )(h, w_in, w_in)


def _residual_body(a_ref, w_ref, r_ref, o_ref, *, scale):
    acc = jnp.dot(a_ref[...], w_ref[...], preferred_element_type=F32)
    o_ref[...] = r_ref[...] + scale * acc


def _residual_proj(a, w, resid, layer, scale, *, bm=1024, bn=512):
    m, k = a.shape
    n = w.shape[2]
    bm, bn = min(bm, m), min(bn, n)
    return pl.pallas_call(
        functools.partial(_residual_body, scale=scale),
        grid=(m // bm, n // bn),
        in_specs=[pl.BlockSpec((bm, k), lambda i, j: (i, 0)),
                  pl.BlockSpec((None, k, bn), lambda i, j: (layer, 0, j)),
                  pl.BlockSpec((bm, bn), lambda i, j: (i, j))],
        out_specs=pl.BlockSpec((bm, bn), lambda i, j: (i, j)),
        out_shape=jax.ShapeDtypeStruct((m, n), F32),
        compiler_params=_params(2),
        name="residual_proj",
    )(a, w, resid)


def _rope_tables(seq):
    def tables(dim):
        inv = 1.0 / (ROPE_THETA ** (jnp.arange(0, dim, 2, dtype=F32) / dim))
        ang = jnp.arange(seq, dtype=F32)[:, None] * inv[None, :]
        cos, sin = jnp.cos(ang), jnp.sin(ang)
        reps = HEAD_DIM // dim
        return (jnp.tile(jnp.concatenate([cos, cos], axis=-1), (1, reps)),
                jnp.tile(jnp.concatenate([-sin, sin], axis=-1), (1, reps)))
    return tables(HEAD_DIM) + tables(DIFF_QK_DIM)


def _qkv_body(h_ref, w_ref, cos_a_ref, sin_a_ref, cos_c_ref, sin_c_ref, o_ref, *, segments, heads_per_block):
    acc = jnp.dot(h_ref[...], w_ref[...], preferred_element_type=F32)
    j = pl.program_id(1)

    def partner_a(x):
        return pltpu.roll(x, HEAD_DIM // 2, axis=1)

    def partner_c(x):
        lane = lax.broadcasted_iota(jnp.int32, x.shape, 1)
        first_half = (lane % DIFF_QK_DIM) < (DIFF_QK_DIM // 2)
        return jnp.where(first_half,
                         pltpu.roll(x, HEAD_DIM - DIFF_QK_DIM // 2, axis=1),
                         pltpu.roll(x, DIFF_QK_DIM // 2, axis=1))

    def emit(rope, scale):
        for s in range(heads_per_block):
            cols = slice(s * HEAD_DIM, (s + 1) * HEAD_DIM)
            x = acc[:, cols]
            if rope == "a":
                x = x * cos_a_ref[...] + partner_a(x) * sin_a_ref[...]
            elif rope == "c":
                x = x * cos_c_ref[...] + partner_c(x) * sin_c_ref[...]
            if scale != 1.0:
                x = x * scale
            o_ref[:, cols] = x.astype(o_ref.dtype)

    for start, stop, rope, scale in segments:
        pl.when((j >= start) & (j < stop))(functools.partial(emit, rope, scale))


def _qkv_proj(h, w_qkv, layer, tables, seq, *, bm=1024, bn=512):
    m, k = h.shape
    n = w_qkv.shape[2]
    bm = min(bm, seq)
    hpb = bn // HEAD_DIM
    scale_a = LOG2_E / math.sqrt(HEAD_DIM)
    scale_c = LOG2_E / math.sqrt(DIFF_QK_DIM)
    layout = [(QA_OFF, "a", scale_a), (KA_OFF, "a", 1.0), (VA_OFF, None, 1.0),
              (QB_OFF, None, scale_a), (KB_OFF, None, 1.0),
              (QC_OFF, "c", scale_c), (KC_OFF, "c", 1.0), (VC_OFF, None, 1.0)]
    bounds = [off for off, _, _ in layout] + [QKV_HEADS]
    assert all(b % hpb == 0 for b in bounds)
    segments = tuple((bounds[t] // hpb, bounds[t + 1] // hpb, rope, scale)
                     for t, (_, rope, scale) in enumerate(layout))
    pos_blocks = seq // bm
    table_spec = pl.BlockSpec((bm, LANES), lambda i, j: (i % pos_blocks, 0))
    return pl.pallas_call(
        functools.partial(_qkv_body, segments=segments, heads_per_block=hpb),
        grid=(m // bm, n // bn),
        in_specs=[pl.BlockSpec((bm, k), lambda i, j: (i, 0)),
                  pl.BlockSpec((None, k, bn), lambda i, j: (layer, 0, j)),
                  table_spec, table_spec, table_spec, table_spec],
        out_specs=pl.BlockSpec((bm, bn), lambda i, j: (i, j)),
        out_shape=jax.ShapeDtypeStruct((m, n), BF16),
        compiler_params=_params(2),
        name="qkv_proj",
    )(h, w_qkv, *tables)


def _swa_body(sinks_ref, q_ref, kp_ref, kc_ref, vp_ref, vc_ref, o_ref):
    i = pl.program_id(1)
    hk = pl.program_id(2)
    blk = WINDOW
    q = q_ref[...]
    qs = jnp.concatenate([q[:, g * HEAD_DIM:(g + 1) * HEAD_DIM] for g in range(SWA_GROUP)], axis=0)
    k = jnp.concatenate([kp_ref[...], kc_ref[...]], axis=0)
    v = jnp.concatenate([vp_ref[...], vc_ref[...]], axis=0)
    s = lax.dot_general(qs, k, (((1,), (1,)), ((), ())), preferred_element_type=F32)
    rows = SWA_GROUP * blk
    t = lax.broadcasted_iota(jnp.int32, (rows, 2 * blk), 0) % blk
    c = lax.broadcasted_iota(jnp.int32, (rows, 2 * blk), 1)
    valid = (c > t) & (c <= t + blk) & ((c >= blk) | (i > 0))
    s = jnp.where(valid, s, MASK_VALUE)
    group = lax.broadcasted_iota(jnp.int32, (rows, 1), 0) // blk
    sink = jnp.zeros((rows, 1), F32)
    for g in range(SWA_GROUP):
        sink = jnp.where(group == g, sinks_ref[hk * SWA_GROUP + g] * LOG2_E, sink)
    m = jnp.maximum(jnp.max(s, axis=-1, keepdims=True), sink)
    p = jnp.exp2(s - m)
    denom = jnp.sum(p, axis=-1, keepdims=True) + jnp.exp2(sink - m)
    o = jnp.dot(p.astype(BF16), v, preferred_element_type=F32) / denom
    for g in range(SWA_GROUP):
        o_ref[:, g * HEAD_DIM:(g + 1) * HEAD_DIM] = o[g * blk:(g + 1) * blk].astype(o_ref.dtype)


def _swa_attention(qkv, sinks, batch, seq):
    m = qkv.shape[0]
    blk = WINDOW
    nq = seq // blk
    gw = SWA_GROUP * HEAD_DIM

    def cur(off):
        return pl.BlockSpec((blk, HEAD_DIM), lambda b, i, hk: (b * nq + i, off + hk))

    def prev(off):
        return pl.BlockSpec((blk, HEAD_DIM), lambda b, i, hk: (b * nq + jnp.maximum(i - 1, 0), off + hk))

    return pl.pallas_call(
        _swa_body,
        grid=(batch, nq, SWA_KV_HEADS),
        in_specs=[pl.BlockSpec(memory_space=pltpu.SMEM),
                  pl.BlockSpec((blk, gw), lambda b, i, hk: (b * nq + i, hk)),
                  prev(KA_OFF), cur(KA_OFF), prev(VA_OFF), cur(VA_OFF)],
        out_specs=pl.BlockSpec((blk, gw), lambda b, i, hk: (b * nq + i, hk)),
        out_shape=jax.ShapeDtypeStruct((m, SWA_Q_HEADS * HEAD_DIM), BF16),
        compiler_params=_params(3),
        name="swa_attention",
    )(sinks, qkv, qkv, qkv, qkv, qkv)


def _stickbreak_body(q_ref, k_ref, v_ref, o_ref, *, blk, heads, chunk):
    i = pl.program_id(2)
    kj = lax.broadcasted_iota(jnp.int32, (blk, blk), 0)
    ks = lax.broadcasted_iota(jnp.int32, (blk, blk), 1)
    later_keys = jnp.where(kj > ks, 1.0, 0.0).astype(BF16)
    causal = ks < kj
    row_chunks = [slice(r, r + chunk) for r in range(0, blk, chunk)]

    def tile(kb, state, diagonal):
        rows = pl.ds(pl.multiple_of(kb * blk, blk), blk)
        head_cols = [slice(c * HEAD_DIM, (c + 1) * HEAD_DIM) for c in range(heads)]
        z = [lax.dot_general(q_ref[:, cols], k_ref[rows, cols], (((1,), (1,)), ((), ())),
                             preferred_element_type=F32) for cols in head_cols]
        log_beta, later, totals = [], [], []
        for c in range(heads):
            lb, hi, lo, tot = [], [], [], []
            for rc in row_chunks:
                zc = z[c][rc]
                softplus = jnp.maximum(zc, 0.0) + jnp.log2(1.0 + jnp.exp2(-jnp.abs(zc)))
                lb.append(zc - softplus)
                if diagonal:
                    softplus = jnp.where(causal[rc], softplus, 0.0)
                hi.append(softplus.astype(BF16))
                lo.append((softplus - hi[-1].astype(F32)).astype(BF16))
                tot.append(jnp.sum(softplus, axis=-1, keepdims=True))
            log_beta.append(lb)
            later.append(jnp.dot(jnp.concatenate(hi, axis=0), later_keys, preferred_element_type=F32)
                         + jnp.dot(jnp.concatenate(lo, axis=0), later_keys, preferred_element_type=F32))
            totals.append(jnp.concatenate(tot, axis=0))
        new_state = []
        for c in range(heads):
            run, acc = state[c]
            a = []
            for n, rc in enumerate(row_chunks):
                ac = jnp.exp2(log_beta[c][n] - (run[rc] + later[c][rc]))
                if diagonal:
                    ac = jnp.where(causal[rc], ac, 0.0)
                a.append(ac.astype(BF16))
            acc = acc + jnp.dot(jnp.concatenate(a, axis=0), v_ref[rows, head_cols[c]],
                                preferred_element_type=F32)
            new_state.append((run + totals[c], acc))
        return tuple(new_state)

    state = tuple((jnp.zeros((blk, 1), F32), jnp.zeros((blk, HEAD_DIM), F32)) for _ in range(heads))
    state = tile(i, state, True)
    state = lax.fori_loop(0, i, lambda t, s: tile(i - 1 - t, s, False), state)
    for c in range(heads):
        o_ref[:, c * HEAD_DIM:(c + 1) * HEAD_DIM] = state[c][1].astype(o_ref.dtype)


def _stickbreak_attention(qkv, batch, seq, *, blk=256, heads=4, chunk=64):
    m = qkv.shape[0]
    blk = min(blk, seq)
    nq = seq // blk
    width = heads * HEAD_DIM
    assert all(off % heads == 0 for off in (QB_OFF, KB_OFF, VB_OFF, SB_HEADS))
    return pl.pallas_call(
        functools.partial(_stickbreak_body, blk=blk, heads=heads, chunk=min(chunk, blk)),
        grid=(batch, SB_HEADS // heads, nq),
        in_specs=[pl.BlockSpec((blk, width), lambda b, h, i: (b * nq + i, QB_OFF // heads + h)),
                  pl.BlockSpec((seq, width), lambda b, h, i: (b, KB_OFF // heads + h)),
                  pl.BlockSpec((seq, width), lambda b, h, i: (b, VB_OFF // heads + h))],
        out_specs=pl.BlockSpec((blk, width), lambda b, h, i: (b * nq + i, h)),
        out_shape=jax.ShapeDtypeStruct((m, SB_HEADS * HEAD_DIM), BF16),
        compiler_params=_params(3),
        name="stickbreak_attention",
    )(qkv, qkv, qkv)


def _diff_body(q_ref, k_ref, v_ref, lam_ref, g_ref, o_ref, vt_ref, *, blk, heads, lam_init):
    i = pl.program_id(2)
    nkb = k_ref.shape[0] // blk
    head_cols = [slice(h * HEAD_DIM, (h + 1) * HEAD_DIM) for h in range(heads)]

    @pl.when(i == 0)
    def _():
        for h in range(heads):
            for kb in range(nkb):
                vt_ref[h, kb] = v_ref[kb * blk:(kb + 1) * blk, head_cols[h]].T

    lane = lax.broadcasted_iota(jnp.int32, (blk, HEAD_DIM), 1)
    key = lax.broadcasted_iota(jnp.int32, (blk, 2 * blk), 0)
    qry = lax.broadcasted_iota(jnp.int32, (blk, 2 * blk), 1) % blk
    causal = key <= qry

    def both_maps(q):
        zero = jnp.zeros_like(q)
        return jnp.concatenate([jnp.where(lane < DIFF_QK_DIM, q, zero),
                                jnp.where(lane >= DIFF_QK_DIM, q, zero)], axis=0)

    qq = [both_maps(q_ref[:, cols]) for cols in head_cols]

    def tile(kb, state, diagonal):
        rows = pl.ds(pl.multiple_of(kb * blk, blk), blk)
        scores = [lax.dot_general(k_ref[rows, head_cols[h]], qq[h], (((1,), (1,)), ((), ())),
                                  preferred_element_type=F32) for h in range(heads)]
        new_state = []
        for h in range(heads):
            m, l, acc = state[h]
            s = jnp.where(causal, scores[h], MASK_VALUE) if diagonal else scores[h]
            m_new = jnp.maximum(m, jnp.max(s, axis=0, keepdims=True))
            alpha = jnp.exp2(m - m_new)
            p = jnp.exp2(s - m_new)
            l = alpha * l + jnp.sum(p, axis=0, keepdims=True)
            acc = alpha * acc + jnp.dot(vt_ref[h, kb], p.astype(BF16), preferred_element_type=F32)
            new_state.append((m_new, l, acc))
        return tuple(new_state)

    state = tuple((jnp.full((1, 2 * blk), MASK_VALUE, F32),
                   jnp.zeros((1, 2 * blk), F32),
                   jnp.zeros((HEAD_DIM, 2 * blk), F32)) for _ in range(heads))
    state = tile(i, state, True)
    state = lax.fori_loop(0, i, lambda kb, s: tile(kb, s, False), state)

    lam_vecs = lam_ref[...]
    lam = (jnp.exp(jnp.sum(lam_vecs[0:1] * lam_vecs[1:2], axis=-1, keepdims=True))
           - jnp.exp(jnp.sum(lam_vecs[2:3] * lam_vecs[3:4], axis=-1, keepdims=True))
           + lam_init)
    for h in range(heads):
        _, l, acc = state[h]
        o = acc / l
        d = o[:, :blk] - lam * o[:, blk:]
        ms = jnp.mean(d * d, axis=0, keepdims=True)
        y = d * lax.rsqrt(ms + DIFF_NORM_EPS) * g_ref[...]
        o_ref[:, head_cols[h]] = (y * (1.0 - lam_init)).T.astype(o_ref.dtype)


def _diff_attention(qkv, lam_vecs, diff_norm, layer, lam_init, batch, seq, *, blk=256, heads=4):
    m = qkv.shape[0]
    blk = min(blk, seq)
    nq = seq // blk
    width = heads * HEAD_DIM
    assert all(off % heads == 0 for off in (QC_OFF, KC_OFF, VC_OFF, DIFF_HEADS))
    return pl.pallas_call(
        functools.partial(_diff_body, blk=blk, heads=heads, lam_init=lam_init),
        grid=(batch, DIFF_HEADS // heads, nq),
        in_specs=[pl.BlockSpec((blk, width), lambda b, h, i: (b * nq + i, QC_OFF // heads + h)),
                  pl.BlockSpec((seq, width), lambda b, h, i: (b, KC_OFF // heads + h)),
                  pl.BlockSpec((seq, width), lambda b, h, i: (b, VC_OFF // heads + h)),
                  pl.BlockSpec(lam_vecs.shape, lambda b, h, i: (0, 0)),
                  pl.BlockSpec((None, HEAD_DIM, 1), lambda b, h, i: (layer, 0, 0))],
        out_specs=pl.BlockSpec((blk, width), lambda b, h, i: (b * nq + i, h)),
        out_shape=jax.ShapeDtypeStruct((m, DIFF_HEADS * HEAD_DIM), BF16),
        scratch_shapes=[pltpu.VMEM((heads, nq, HEAD_DIM, blk), BF16)],
        compiler_params=_params(3, sequential_last=True),
        name="diff_attention",
    )(qkv, qkv, qkv, lam_vecs, diff_norm)


def _merge_body(h_ref, wga_ref, wgb_ref, wgc_ref, oa_ref, wa_ref, ob_ref, wb_ref, oc_ref, wc_ref, o_ref):
    h = h_ref[...]

    def gated(wg_ref, x_ref, w_ref):
        gate = jax.nn.sigmoid(jnp.dot(h, wg_ref[...], preferred_element_type=F32))
        return gate * jnp.dot(x_ref[...], w_ref[...], preferred_element_type=F32)

    merged = gated(wga_ref, oa_ref, wa_ref) + gated(wgb_ref, ob_ref, wb_ref) + gated(wgc_ref, oc_ref, wc_ref)
    o_ref[...] = merged.astype(o_ref.dtype)


def _gated_merge(h, w_gate, out_a, w_a, out_b, w_b, out_c, w_c, layer, *, bm=512, bn=256):
    m, k = h.shape
    d = w_a.shape[2]
    bm, bn = min(bm, m), min(bn, d)
    nj = d // bn

    def act(x):
        return pl.BlockSpec((bm, x.shape[1]), lambda i, j: (i, 0))

    def weight(w, block_off=0):
        return pl.BlockSpec((None, w.shape[1], bn), lambda i, j: (layer, 0, j + block_off))

    return pl.pallas_call(
        _merge_body,
        grid=(m // bm, nj),
        in_specs=[act(h), weight(w_gate), weight(w_gate, nj), weight(w_gate, 2 * nj),
                  act(out_a), weight(w_a), act(out_b), weight(w_b), act(out_c), weight(w_c)],
        out_specs=pl.BlockSpec((bm, bn), lambda i, j: (i, j)),
        out_shape=jax.ShapeDtypeStruct((m, d), BF16),
        compiler_params=_params(2),
        name="gated_merge",
    )(h, w_gate, w_gate, w_gate, out_a, w_a, out_b, w_b, out_c, w_c)


def kernel(x, ffn1_norm, ffn1_w_in, ffn1_w_out, mix_norm, w_qkv, w_gate, sinks, lambda_q1, lambda_k1, lambda_q2, lambda_k2, diff_norm, w_branch_a, w_branch_b, w_branch_c, w_out, ffn2_norm, ffn2_w_in, ffn2_w_out, final_norm):
    batch, seq, d_model = x.shape
    depth = ffn1_w_in.shape[0]
    assert w_qkv.shape[2] == QKV_HEADS * HEAD_DIM
    m = batch * seq

    def weights(w):
        return w.astype(BF16)

    def gains(g):
        return g.astype(F32).reshape(g.shape[0], 1, g.shape[1])

    ffn1_w_in, ffn1_w_out, ffn2_w_in, ffn2_w_out = map(weights, (ffn1_w_in, ffn1_w_out, ffn2_w_in, ffn2_w_out))
    w_qkv, w_gate, w_out = map(weights, (w_qkv, w_gate, w_out))
    w_branch_a, w_branch_b, w_branch_c = map(weights, (w_branch_a, w_branch_b, w_branch_c))
    ffn1_norm, mix_norm, ffn2_norm = map(gains, (ffn1_norm, mix_norm, ffn2_norm))
    diff_norm = diff_norm.astype(F32).reshape(depth, HEAD_DIM, 1)
    final_gain = final_norm.astype(F32).reshape(1, 1, d_model)
    tables = _rope_tables(seq)
    sinks = sinks.astype(F32)
    lam_all = jnp.stack([lambda_q1, lambda_k1, lambda_q2, lambda_k2], axis=1).astype(F32)

    x = x.reshape(m, d_model)
    for layer in range(depth):
        h = _rmsnorm(x, ffn1_norm, layer, BF16)
        x = _residual_proj(_swiglu_up(h, ffn1_w_in, layer), ffn1_w_out, x, layer, 0.5)

        h = _rmsnorm(x, mix_norm, layer, BF16)
        qkv = _qkv_proj(h, w_qkv, layer, tables, seq)
        out_a = _swa_attention(qkv, sinks[layer], batch, seq)
        out_b = _stickbreak_attention(qkv, batch, seq)
        lam_init = 0.8 - 0.6 * math.exp(-0.3 * layer)
        out_c = _diff_attention(qkv, lam_all[layer], diff_norm, layer, lam_init, batch, seq)
        merged = _gated_merge(h, w_gate, out_a, w_branch_a, out_b, w_branch_b, out_c, w_branch_c, layer)
        x = _residual_proj(merged, w_out, x, layer, 1.0)

        h = _rmsnorm(x, ffn2_norm, layer, BF16)
        x = _residual_proj(_swiglu_up(h, ffn2_w_in, layer), ffn2_w_out, x, layer, 0.5)

    out = _rmsnorm(x, final_gain, 0, F32)
    return out.reshape(batch, seq, d_model)
```

```python
import functools
import math

import jax
import jax.numpy as jnp
from jax import lax
from jax.experimental import pallas as pl
from jax.experimental.pallas import tpu as pltpu

F32 = jnp.float32
BF16 = jnp.bfloat16

HEAD_DIM = 128
ROPE_THETA = 10000.0
EPS = 1e-6
SWA_Q_HEADS = 16
SWA_KV_HEADS = 4
SWA_GROUP = SWA_Q_HEADS // SWA_KV_HEADS
WINDOW = 128
SB_HEADS = 8
DIFF_HEADS = 8
DIFF_QK_DIM = HEAD_DIM // 2
DIFF_NORM_EPS = 1e-5
QA_OFF = 0
KA_OFF = QA_OFF + SWA_Q_HEADS
VA_OFF = KA_OFF + SWA_KV_HEADS
QB_OFF = VA_OFF + SWA_KV_HEADS
KB_OFF = QB_OFF + SB_HEADS
VB_OFF = KB_OFF + SB_HEADS
QC_OFF = VB_OFF + SB_HEADS
KC_OFF = QC_OFF + DIFF_HEADS
VC_OFF = KC_OFF + DIFF_HEADS
QKV_HEADS = VC_OFF + DIFF_HEADS

V7X_VMEM_BYTES = 64 * 1024 * 1024
VMEM_LIMIT_BYTES = V7X_VMEM_BYTES * 3 // 4
LANES = 128
MASK_VALUE = -1e30
LOG2_E = math.log2(math.e)


def _params(n_grid_dims, sequential_last=False):
    semantics = ("parallel",) * n_grid_dims
    if sequential_last:
        semantics = semantics[:-1] + ("arbitrary",)
    return pltpu.CompilerParams(dimension_semantics=semantics, vmem_limit_bytes=VMEM_LIMIT_BYTES)


def _resident_spec(block_shape, index_map):
    return pl.BlockSpec(block_shape, index_map, pipeline_mode=pl.Buffered(1))


def _rmsnorm_body(x_ref, g_ref, o_ref, *, eps):
    x = x_ref[...]
    ms = jnp.mean(x * x, axis=-1, keepdims=True)
    o_ref[...] = (x * lax.rsqrt(ms + eps) * g_ref[...]).astype(o_ref.dtype)


def _rmsnorm(x, gains, layer, out_dtype, *, block_rows=256):
    m, d = x.shape
    br = min(block_rows, m)
    return pl.pallas_call(
        functools.partial(_rmsnorm_body, eps=EPS),
        grid=(m // br,),
        in_specs=[pl.BlockSpec((br, d), lambda i: (i, 0)),
                  pl.BlockSpec((None, 1, d), lambda i: (layer, 0, 0))],
        out_specs=pl.BlockSpec((br, d), lambda i: (i, 0)),
        out_shape=jax.ShapeDtypeStruct((m, d), out_dtype),
        compiler_params=_params(1),
        name="rmsnorm",
    )(x, gains)


def _swiglu_body(h_ref, wg_ref, wu_ref, o_ref):
    h = h_ref[...]
    g = jnp.dot(h, wg_ref[...], preferred_element_type=F32)
    u = jnp.dot(h, wu_ref[...], preferred_element_type=F32)
    o_ref[...] = (g * jax.nn.sigmoid(g) * u).astype(o_ref.dtype)


def _swiglu_up(h, w_in, layer, *, bm=1024, bn=512):
    m, k = h.shape
    f = w_in.shape[2] // 2
    bm, bn = min(bm, m), min(bn, f)
    nj = f // bn
    return pl.pallas_call(
        _swiglu_body,
        grid=(m // bm, nj),
        in_specs=[pl.BlockSpec((bm, k), lambda i, j: (i, 0)),
                  pl.BlockSpec((None, k, bn), lambda i, j: (layer, 0, j)),
                  pl.BlockSpec((None, k, bn), lambda i, j: (layer, 0, j + nj))],
        out_specs=pl.BlockSpec((bm, bn), lambda i, j: (i, j)),
        out_shape=jax.ShapeDtypeStruct((m, f), BF16),
        compiler_params=_params(2),
        name="swiglu_up",
    )(h, w_in, w_in)


def _residual_body(a_ref, w_ref, r_ref, o_ref, *, scale):
    acc = jnp.dot(a_ref[...], w_ref[...], preferred_element_type=F32)
    o_ref[...] = r_ref[...] + scale * acc


def _residual_proj(a, w, resid, layer, scale, *, bm=1024, bn=512):
    m, k = a.shape
    n = w.shape[2]
    bm, bn = min(bm, m), min(bn, n)
    return pl.pallas_call(
        functools.partial(_residual_body, scale=scale),
        grid=(m // bm, n // bn),
        in_specs=[pl.BlockSpec((bm, k), lambda i, j: (i, 0)),
                  pl.BlockSpec((None, k, bn), lambda i, j: (layer, 0, j)),
                  pl.BlockSpec((bm, bn), lambda i, j: (i, j))],
        out_specs=pl.BlockSpec((bm, bn), lambda i, j: (i, j)),
        out_shape=jax.ShapeDtypeStruct((m, n), F32),
        compiler_params=_params(2),
        name="residual_proj",
    )(a, w, resid)


def _rope_tables(seq):
    def tables(dim):
        inv = 1.0 / (ROPE_THETA ** (jnp.arange(0, dim, 2, dtype=F32) / dim))
        ang = jnp.arange(seq, dtype=F32)[:, None] * inv[None, :]
        cos, sin = jnp.cos(ang), jnp.sin(ang)
        reps = HEAD_DIM // dim
        return (jnp.tile(jnp.concatenate([cos, cos], axis=-1), (1, reps)),
                jnp.tile(jnp.concatenate([-sin, sin], axis=-1), (1, reps)))
    return tables(HEAD_DIM) + tables(DIFF_QK_DIM)


def _qkv_body(h_ref, w_ref, cos_a_ref, sin_a_ref, cos_c_ref, sin_c_ref, o_ref, *, segments, heads_per_block):
    acc = jnp.dot(h_ref[...], w_ref[...], preferred_element_type=F32)
    j = pl.program_id(1)

    def partner_a(x):
        return pltpu.roll(x, HEAD_DIM // 2, axis=1)

    def partner_c(x):
        lane = lax.broadcasted_iota(jnp.int32, x.shape, 1)
        first_half = (lane % DIFF_QK_DIM) < (DIFF_QK_DIM // 2)
        return jnp.where(first_half,
                         pltpu.roll(x, HEAD_DIM - DIFF_QK_DIM // 2, axis=1),
                         pltpu.roll(x, DIFF_QK_DIM // 2, axis=1))

    def emit(rope, scale):
        for s in range(heads_per_block):
            cols = slice(s * HEAD_DIM, (s + 1) * HEAD_DIM)
            x = acc[:, cols]
            if rope == "a":
                x = x * cos_a_ref[...] + partner_a(x) * sin_a_ref[...]
            elif rope == "c":
                x = x * cos_c_ref[...] + partner_c(x) * sin_c_ref[...]
            if scale != 1.0:
                x = x * scale
            o_ref[:, cols] = x.astype(o_ref.dtype)

    for start, stop, rope, scale in segments:
        pl.when((j >= start) & (j < stop))(functools.partial(emit, rope, scale))


def _qkv_proj(h, w_qkv, layer, tables, seq, *, bm=1024, bn=512):
    m, k = h.shape
    n = w_qkv.shape[2]
    bm = min(bm, seq)
    hpb = bn // HEAD_DIM
    scale_a = LOG2_E / math.sqrt(HEAD_DIM)
    scale_c = LOG2_E / math.sqrt(DIFF_QK_DIM)
    layout = [(QA_OFF, "a", scale_a), (KA_OFF, "a", 1.0), (VA_OFF, None, 1.0),
              (QB_OFF, None, scale_a), (KB_OFF, None, 1.0),
              (QC_OFF, "c", scale_c), (KC_OFF, "c", 1.0), (VC_OFF, None, 1.0)]
    bounds = [off for off, _, _ in layout] + [QKV_HEADS]
    assert all(b % hpb == 0 for b in bounds)
    segments = tuple((bounds[t] // hpb, bounds[t + 1] // hpb, rope, scale)
                     for t, (_, rope, scale) in enumerate(layout))
    pos_blocks = seq // bm
    table_spec = pl.BlockSpec((bm, LANES), lambda i, j: (i % pos_blocks, 0))
    return pl.pallas_call(
        functools.partial(_qkv_body, segments=segments, heads_per_block=hpb),
        grid=(m // bm, n // bn),
        in_specs=[pl.BlockSpec((bm, k), lambda i, j: (i, 0)),
                  pl.BlockSpec((None, k, bn), lambda i, j: (layer, 0, j)),
                  table_spec, table_spec, table_spec, table_spec],
        out_specs=pl.BlockSpec((bm, bn), lambda i, j: (i, j)),
        out_shape=jax.ShapeDtypeStruct((m, n), BF16),
        compiler_params=_params(2),
        name="qkv_proj",
    )(h, w_qkv, *tables)


def _swa_body(sinks_ref, q_ref, kp_ref, kc_ref, vp_ref, vc_ref, o_ref):
    i = pl.program_id(1)
    blk = WINDOW
    rows = SWA_GROUP * blk
    t = lax.broadcasted_iota(jnp.int32, (rows, 2 * blk), 0) % blk
    c = lax.broadcasted_iota(jnp.int32, (rows, 2 * blk), 1)
    valid = (c > t) & (c <= t + blk) & ((c >= blk) | (i > 0))
    group = lax.broadcasted_iota(jnp.int32, (rows, 1), 0) // blk

    def head_cols(h):
        return slice(h * HEAD_DIM, (h + 1) * HEAD_DIM)

    scores = []
    for hk in range(SWA_KV_HEADS):
        qs = jnp.concatenate([q_ref[:, head_cols(hk * SWA_GROUP + g)] for g in range(SWA_GROUP)], axis=0)
        k = jnp.concatenate([kp_ref[:, head_cols(hk)], kc_ref[:, head_cols(hk)]], axis=0)
        scores.append(lax.dot_general(qs, k, (((1,), (1,)), ((), ())), preferred_element_type=F32))
    for hk in range(SWA_KV_HEADS):
        s = jnp.where(valid, scores[hk], MASK_VALUE)
        sink = jnp.zeros((rows, 1), F32)
        for g in range(SWA_GROUP):
            sink = jnp.where(group == g, sinks_ref[hk * SWA_GROUP + g] * LOG2_E, sink)
        m = jnp.maximum(jnp.max(s, axis=-1, keepdims=True), sink)
        p = jnp.exp2(s - m)
        denom = jnp.sum(p, axis=-1, keepdims=True) + jnp.exp2(sink - m)
        v = jnp.concatenate([vp_ref[:, head_cols(hk)], vc_ref[:, head_cols(hk)]], axis=0)
        o = jnp.dot(p.astype(BF16), v, preferred_element_type=F32) / denom
        for g in range(SWA_GROUP):
            o_ref[:, head_cols(hk * SWA_GROUP + g)] = o[g * blk:(g + 1) * blk].astype(o_ref.dtype)


def _swa_attention(qkv, sinks, batch, seq):
    m = qkv.shape[0]
    blk = WINDOW
    nq = seq // blk
    qw = SWA_Q_HEADS * HEAD_DIM
    kw = SWA_KV_HEADS * HEAD_DIM
    assert QA_OFF % SWA_Q_HEADS == 0 and KA_OFF % SWA_KV_HEADS == 0 and VA_OFF % SWA_KV_HEADS == 0

    def cur(off):
        return pl.BlockSpec((blk, kw), lambda b, i: (b * nq + i, off // SWA_KV_HEADS))

    def prev(off):
        return pl.BlockSpec((blk, kw), lambda b, i: (b * nq + jnp.maximum(i - 1, 0), off // SWA_KV_HEADS))

    return pl.pallas_call(
        _swa_body,
        grid=(batch, nq),
        in_specs=[pl.BlockSpec(memory_space=pltpu.SMEM),
                  pl.BlockSpec((blk, qw), lambda b, i: (b * nq + i, QA_OFF // SWA_Q_HEADS)),
                  prev(KA_OFF), cur(KA_OFF), prev(VA_OFF), cur(VA_OFF)],
        out_specs=pl.BlockSpec((blk, qw), lambda b, i: (b * nq + i, 0)),
        out_shape=jax.ShapeDtypeStruct((m, qw), BF16),
        compiler_params=_params(2),
        name="swa_attention",
    )(sinks, qkv, qkv, qkv, qkv, qkv)


def _stickbreak_body(q_ref, k_ref, v_ref, o_ref, vt_ref, acc_ref, *, blk, heads):
    i = pl.program_id(2)
    nkb = k_ref.shape[0] // blk
    head_cols = [slice(c * HEAD_DIM, (c + 1) * HEAD_DIM) for c in range(heads)]

    @pl.when(i == 0)
    def _():
        for c in range(heads):
            for kb in range(nkb):
                vt_ref[c, kb] = v_ref[kb * blk:(kb + 1) * blk, head_cols[c]].T

    row = lax.broadcasted_iota(jnp.int32, (blk, blk), 0)
    col = lax.broadcasted_iota(jnp.int32, (blk, blk), 1)
    from_key = jnp.where(col >= row, 1.0, 0.0).astype(BF16)
    causal = row < col

    def tile(kb, run, diagonal):
        rows = pl.ds(pl.multiple_of(kb * blk, blk), blk)
        z = [lax.dot_general(k_ref[rows, cols], q_ref[:, cols], (((1,), (1,)), ((), ())),
                             preferred_element_type=F32) for cols in head_cols]
        suffix = []
        for c in range(heads):
            softplus = jnp.maximum(z[c], 0.0) + jnp.log2(1.0 + jnp.exp2(-jnp.abs(z[c])))
            if diagonal:
                softplus = jnp.where(causal, softplus, 0.0)
            hi = softplus.astype(BF16)
            lo = (softplus - hi.astype(F32)).astype(BF16)
            suffix.append(jnp.dot(from_key, hi, preferred_element_type=F32)
                          + jnp.dot(from_key, lo, preferred_element_type=F32))
        new_run = []
        for c in range(heads):
            a = jnp.exp2(z[c] - (run[c] + suffix[c]))
            if diagonal:
                a = jnp.where(causal, a, 0.0)
            pv = jnp.dot(vt_ref[c, kb], a.astype(BF16), preferred_element_type=F32)
            acc_ref[c] = pv if diagonal else acc_ref[c] + pv
            new_run.append(run[c] + suffix[c][0:1])
        return tuple(new_run)

    run = tile(i, tuple(jnp.zeros((1, blk), F32) for _ in range(heads)), True)
    lax.fori_loop(0, i, lambda t, r: tile(i - 1 - t, r, False), run)
    for c in range(heads):
        o_ref[:, head_cols[c]] = acc_ref[c].T.astype(o_ref.dtype)


def _stickbreak_attention(qkv, batch, seq, *, blk=256, heads=8):
    m = qkv.shape[0]
    blk = min(blk, seq)
    nq = seq // blk
    width = heads * HEAD_DIM
    assert all(off % heads == 0 for off in (QB_OFF, KB_OFF, VB_OFF, SB_HEADS))
    return pl.pallas_call(
        functools.partial(_stickbreak_body, blk=blk, heads=heads),
        grid=(batch, SB_HEADS // heads, nq),
        in_specs=[pl.BlockSpec((blk, width), lambda b, h, i: (b * nq + i, QB_OFF // heads + h)),
                  _resident_spec((seq, width), lambda b, h, i: (b, KB_OFF // heads + h)),
                  _resident_spec((seq, width), lambda b, h, i: (b, VB_OFF // heads + h))],
        out_specs=pl.BlockSpec((blk, width), lambda b, h, i: (b * nq + i, h)),
        out_shape=jax.ShapeDtypeStruct((m, SB_HEADS * HEAD_DIM), BF16),
        scratch_shapes=[pltpu.VMEM((heads, nq, HEAD_DIM, blk), BF16),
                        pltpu.VMEM((heads, HEAD_DIM, blk), F32)],
        compiler_params=_params(3, sequential_last=True),
        name="stickbreak_attention",
    )(qkv, qkv, qkv)


def _diff_body(q_ref, k_ref, v_ref, lam_ref, g_ref, o_ref, vt_ref, acc_ref, *, blk, heads, lam_init):
    i = pl.program_id(2)
    nkb = k_ref.shape[0] // blk
    head_cols = [slice(h * HEAD_DIM, (h + 1) * HEAD_DIM) for h in range(heads)]

    @pl.when(i == 0)
    def _():
        for h in range(heads):
            for kb in range(nkb):
                vt_ref[h, kb] = v_ref[kb * blk:(kb + 1) * blk, head_cols[h]].T

    lane = lax.broadcasted_iota(jnp.int32, (blk, HEAD_DIM), 1)
    key = lax.broadcasted_iota(jnp.int32, (blk, 2 * blk), 0)
    qry = lax.broadcasted_iota(jnp.int32, (blk, 2 * blk), 1) % blk
    causal = key <= qry

    def both_maps(q):
        zero = jnp.zeros_like(q)
        return jnp.concatenate([jnp.where(lane < DIFF_QK_DIM, q, zero),
                                jnp.where(lane >= DIFF_QK_DIM, q, zero)], axis=0)

    qq = [both_maps(q_ref[:, cols]) for cols in head_cols]

    def tile(kb, state, diagonal):
        rows = pl.ds(pl.multiple_of(kb * blk, blk), blk)
        scores = [lax.dot_general(k_ref[rows, head_cols[h]], qq[h], (((1,), (1,)), ((), ())),
                                  preferred_element_type=F32) for h in range(heads)]
        new_state = []
        for h in range(heads):
            m, l = state[h]
            s = jnp.where(causal, scores[h], MASK_VALUE) if diagonal else scores[h]
            m_new = jnp.maximum(m, jnp.max(s, axis=0, keepdims=True))
            alpha = jnp.exp2(m - m_new)
            p = jnp.exp2(s - m_new)
            pv = jnp.dot(vt_ref[h, kb], p.astype(BF16), preferred_element_type=F32)
            acc_ref[h] = pv if diagonal else alpha * acc_ref[h] + pv
            new_state.append((m_new, alpha * l + jnp.sum(p, axis=0, keepdims=True)))
        return tuple(new_state)

    state = tuple((jnp.full((1, 2 * blk), MASK_VALUE, F32), jnp.zeros((1, 2 * blk), F32))
                  for _ in range(heads))
    state = tile(i, state, True)
    state = lax.fori_loop(0, i, lambda kb, s: tile(kb, s, False), state)

    lam_vecs = lam_ref[...]
    lam = (jnp.exp(jnp.sum(lam_vecs[0:1] * lam_vecs[1:2], axis=-1, keepdims=True))
           - jnp.exp(jnp.sum(lam_vecs[2:3] * lam_vecs[3:4], axis=-1, keepdims=True))
           + lam_init)
    for h in range(heads):
        o = acc_ref[h] / state[h][1]
        d = o[:, :blk] - lam * o[:, blk:]
        ms = jnp.mean(d * d, axis=0, keepdims=True)
        y = d * lax.rsqrt(ms + DIFF_NORM_EPS) * g_ref[...]
        o_ref[:, head_cols[h]] = (y * (1.0 - lam_init)).T.astype(o_ref.dtype)


def _diff_attention(qkv, lam_vecs, diff_norm, layer, lam_init, batch, seq, *, blk=512, heads=4):
    m = qkv.shape[0]
    blk = min(blk, seq)
    nq = seq // blk
    width = heads * HEAD_DIM
    assert all(off % heads == 0 for off in (QC_OFF, KC_OFF, VC_OFF, DIFF_HEADS))
    return pl.pallas_call(
        functools.partial(_diff_body, blk=blk, heads=heads, lam_init=lam_init),
        grid=(batch, DIFF_HEADS // heads, nq),
        in_specs=[pl.BlockSpec((blk, width), lambda b, h, i: (b * nq + i, QC_OFF // heads + h)),
                  _resident_spec((seq, width), lambda b, h, i: (b, KC_OFF // heads + h)),
                  _resident_spec((seq, width), lambda b, h, i: (b, VC_OFF // heads + h)),
                  pl.BlockSpec(lam_vecs.shape, lambda b, h, i: (0, 0)),
                  pl.BlockSpec((None, HEAD_DIM, 1), lambda b, h, i: (layer, 0, 0))],
        out_specs=pl.BlockSpec((blk, width), lambda b, h, i: (b * nq + i, h)),
        out_shape=jax.ShapeDtypeStruct((m, DIFF_HEADS * HEAD_DIM), BF16),
        scratch_shapes=[pltpu.VMEM((heads, nq, HEAD_DIM, blk), BF16),
                        pltpu.VMEM((heads, HEAD_DIM, 2 * blk), F32)],
        compiler_params=_params(3, sequential_last=True),
        name="diff_attention",
    )(qkv, qkv, qkv, lam_vecs, diff_norm)


def _merge_body(h_ref, wga_ref, wgb_ref, wgc_ref, oa_ref, wa_ref, ob_ref, wb_ref, oc_ref, wc_ref, o_ref):
    h = h_ref[...]

    def gated(wg_ref, x_ref, w_ref):
        gate = jax.nn.sigmoid(jnp.dot(h, wg_ref[...], preferred_element_type=F32))
        return gate * jnp.dot(x_ref[...], w_ref[...], preferred_element_type=F32)

    merged = gated(wga_ref, oa_ref, wa_ref) + gated(wgb_ref, ob_ref, wb_ref) + gated(wgc_ref, oc_ref, wc_ref)
    o_ref[...] = merged.astype(o_ref.dtype)


def _gated_merge(h, w_gate, out_a, w_a, out_b, w_b, out_c, w_c, layer, *, bm=512, bn=256):
    m, k = h.shape
    d = w_a.shape[2]
    bm, bn = min(bm, m), min(bn, d)
    nj = d // bn

    def act(x):
        return pl.BlockSpec((bm, x.shape[1]), lambda i, j: (i, 0))

    def weight(w, block_off=0):
        return pl.BlockSpec((None, w.shape[1], bn), lambda i, j: (layer, 0, j + block_off))

    return pl.pallas_call(
        _merge_body,
        grid=(m // bm, nj),
        in_specs=[act(h), weight(w_gate), weight(w_gate, nj), weight(w_gate, 2 * nj),
                  act(out_a), weight(w_a), act(out_b), weight(w_b), act(out_c), weight(w_c)],
        out_specs=pl.BlockSpec((bm, bn), lambda i, j: (i, j)),
        out_shape=jax.ShapeDtypeStruct((m, d), BF16),
        compiler_params=_params(2),
        name="gated_merge",
    )(h, w_gate, w_gate, w_gate, out_a, w_a, out_b, w_b, out_c, w_c)


def kernel(x, ffn1_norm, ffn1_w_in, ffn1_w_out, mix_norm, w_qkv, w_gate, sinks, lambda_q1, lambda_k1, lambda_q2, lambda_k2, diff_norm, w_branch_a, w_branch_b, w_branch_c, w_out, ffn2_norm, ffn2_w_in, ffn2_w_out, final_norm):
    batch, seq, d_model = x.shape
    depth = ffn1_w_in.shape[0]
    assert w_qkv.shape[2] == QKV_HEADS * HEAD_DIM
    m = batch * seq

    def weights(w):
        return w.astype(BF16)

    def gains(g):
        return g.astype(F32).reshape(g.shape[0], 1, g.shape[1])

    ffn1_w_in, ffn1_w_out, ffn2_w_in, ffn2_w_out = map(weights, (ffn1_w_in, ffn1_w_out, ffn2_w_in, ffn2_w_out))
    w_qkv, w_gate, w_out = map(weights, (w_qkv, w_gate, w_out))
    w_branch_a, w_branch_b, w_branch_c = map(weights, (w_branch_a, w_branch_b, w_branch_c))
    ffn1_norm, mix_norm, ffn2_norm = map(gains, (ffn1_norm, mix_norm, ffn2_norm))
    diff_norm = diff_norm.astype(F32).reshape(depth, HEAD_DIM, 1)
    final_gain = final_norm.astype(F32).reshape(1, 1, d_model)
    tables = _rope_tables(seq)
    sinks = sinks.astype(F32)
    lam_all = jnp.stack([lambda_q1, lambda_k1, lambda_q2, lambda_k2], axis=1).astype(F32)

    x = x.reshape(m, d_model)
    for layer in range(depth):
        h = _rmsnorm(x, ffn1_norm, layer, BF16)
        x = _residual_proj(_swiglu_up(h, ffn1_w_in, layer), ffn1_w_out, x, layer, 0.5)

        h = _rmsnorm(x, mix_norm, layer, BF16)
        qkv = _qkv_proj(h, w_qkv, layer, tables, seq)
        out_a = _swa_attention(qkv, sinks[layer], batch, seq)
        out_b = _stickbreak_attention(qkv, batch, seq)
        lam_init = 0.8 - 0.6 * math.exp(-0.3 * layer)
        out_c = _diff_attention(qkv, lam_all[layer], diff_norm, layer, lam_init, batch, seq)
        merged = _gated_merge(h, w_gate, out_a, w_branch_a, out_b, w_branch_b, out_c, w_branch_c, layer)
        x = _residual_proj(merged, w_out, x, layer, 1.0)

        h = _rmsnorm(x, ffn2_norm, layer, BF16)
        x = _residual_proj(_swiglu_up(h, ffn2_w_in, layer), ffn2_w_out, x, layer, 0.5)

    out = _rmsnorm(x, final_gain, 0, F32)
    return out.reshape(batch, seq, d_model)
```

```python
import functools
import math

import jax
import jax.numpy as jnp
from jax import lax
from jax.experimental import pallas as pl
from jax.experimental.pallas import tpu as pltpu

F32 = jnp.float32
BF16 = jnp.bfloat16

HEAD_DIM = 128
ROPE_THETA = 10000.0
EPS = 1e-6
SWA_Q_HEADS = 16
SWA_KV_HEADS = 4
SWA_GROUP = SWA_Q_HEADS // SWA_KV_HEADS
WINDOW = 128
SB_HEADS = 8
DIFF_HEADS = 8
DIFF_QK_DIM = HEAD_DIM // 2
DIFF_NORM_EPS = 1e-5
QA_OFF = 0
KA_OFF = QA_OFF + SWA_Q_HEADS
VA_OFF = KA_OFF + SWA_KV_HEADS
QB_OFF = VA_OFF + SWA_KV_HEADS
KB_OFF = QB_OFF + SB_HEADS
VB_OFF = KB_OFF + SB_HEADS
QC_OFF = VB_OFF + SB_HEADS
KC_OFF = QC_OFF + DIFF_HEADS
VC_OFF = KC_OFF + DIFF_HEADS
QKV_HEADS = VC_OFF + DIFF_HEADS

V7X_VMEM_BYTES = 64 * 1024 * 1024
VMEM_LIMIT_BYTES = V7X_VMEM_BYTES * 3 // 4
LANES = 128
MASK_VALUE = -1e30
LOG2_E = math.log2(math.e)


def _params(n_grid_dims, sequential_last=False):
    semantics = ("parallel",) * n_grid_dims
    if sequential_last:
        semantics = semantics[:-1] + ("arbitrary",)
    return pltpu.CompilerParams(dimension_semantics=semantics, vmem_limit_bytes=VMEM_LIMIT_BYTES)


def _resident_spec(block_shape, index_map):
    return pl.BlockSpec(block_shape, index_map, pipeline_mode=pl.Buffered(1))


def _inv_rms(ssq_ref, width, eps=EPS):
    return lax.rsqrt(ssq_ref[...] * (1.0 / width) + eps)


def _rmsnorm_body(x_ref, g_ref, o_ref, *, eps):
    x = x_ref[...]
    ms = jnp.mean(x * x, axis=-1, keepdims=True)
    o_ref[...] = (x * lax.rsqrt(ms + eps) * g_ref[...]).astype(o_ref.dtype)


def _rmsnorm(x, gain, *, block_rows=256):
    m, d = x.shape
    br = min(block_rows, m)
    return pl.pallas_call(
        functools.partial(_rmsnorm_body, eps=EPS),
        grid=(m // br,),
        in_specs=[pl.BlockSpec((br, d), lambda i: (i, 0)),
                  pl.BlockSpec((1, d), lambda i: (0, 0))],
        out_specs=pl.BlockSpec((br, d), lambda i: (i, 0)),
        out_shape=jax.ShapeDtypeStruct((m, d), F32),
        compiler_params=_params(1),
        name="rmsnorm",
    )(x, gain)


def _normed_operand_body(x_ref, g_ref, xg_ref, ssq_ref):
    x = x_ref[...]
    xg_ref[...] = (x * g_ref[...]).astype(xg_ref.dtype)
    ssq_ref[...] = jnp.sum(x * x, axis=-1, keepdims=True)


def _normed_operand(x, gains, layer, *, block_rows=256):
    m, d = x.shape
    br = min(block_rows, m)
    return pl.pallas_call(
        _normed_operand_body,
        grid=(m // br,),
        in_specs=[pl.BlockSpec((br, d), lambda i: (i, 0)),
                  pl.BlockSpec((None, 1, d), lambda i: (layer, 0, 0))],
        out_specs=[pl.BlockSpec((br, d), lambda i: (i, 0)),
                   pl.BlockSpec((br, 1), lambda i: (i, 0))],
        out_shape=[jax.ShapeDtypeStruct((m, d), BF16), jax.ShapeDtypeStruct((m, 1), F32)],
        compiler_params=_params(1),
        name="normed_operand",
    )(x, gains)


def _swiglu_body(h_ref, ssq_ref, wg_ref, wu_ref, o_ref):
    h = h_ref[...]
    inv = _inv_rms(ssq_ref, h.shape[1])
    g = jnp.dot(h, wg_ref[...], preferred_element_type=F32) * inv
    u = jnp.dot(h, wu_ref[...], preferred_element_type=F32) * inv
    o_ref[...] = (g * jax.nn.sigmoid(g) * u).astype(o_ref.dtype)


def _swiglu_up(normed, w_in, layer, *, bm=1024, bn=512):
    h, ssq = normed
    m, k = h.shape
    f = w_in.shape[2] // 2
    bm, bn = min(bm, m), min(bn, f)
    nj = f // bn
    return pl.pallas_call(
        _swiglu_body,
        grid=(m // bm, nj),
        in_specs=[pl.BlockSpec((bm, k), lambda i, j: (i, 0)),
                  pl.BlockSpec((bm, 1), lambda i, j: (i, 0)),
                  pl.BlockSpec((None, k, bn), lambda i, j: (layer, 0, j)),
                  pl.BlockSpec((None, k, bn), lambda i, j: (layer, 0, j + nj))],
        out_specs=pl.BlockSpec((bm, bn), lambda i, j: (i, j)),
        out_shape=jax.ShapeDtypeStruct((m, f), BF16),
        compiler_params=_params(2),
        name="swiglu_up",
    )(h, ssq, w_in, w_in)


def _residual_body(a_ref, w_ref, r_ref, *rest, scale, emit_normed):
    x = r_ref[...] + scale * jnp.dot(a_ref[...], w_ref[...], preferred_element_type=F32)
    if not emit_normed:
        (o_ref,) = rest
        o_ref[...] = x
        return
    g_ref, o_ref, xg_ref, ssq_ref = rest
    o_ref[...] = x
    xg_ref[...] = (x * g_ref[...]).astype(xg_ref.dtype)

    @pl.when(pl.program_id(1) == 0)
    def _():
        ssq_ref[...] = jnp.zeros_like(ssq_ref)

    ssq_ref[...] += jnp.sum(x * x, axis=-1, keepdims=True)


def _residual_proj(a, w, resid, layer, scale, next_gains=None, next_layer=None, *, bm=1024, bn=512):
    m, k = a.shape
    n = w.shape[2]
    bm, bn = min(bm, m), min(bn, n)
    emit_normed = next_gains is not None
    tile = pl.BlockSpec((bm, bn), lambda i, j: (i, j))
    in_specs = [pl.BlockSpec((bm, k), lambda i, j: (i, 0)),
                pl.BlockSpec((None, k, bn), lambda i, j: (layer, 0, j)),
                tile]
    out_specs, out_shape, operands = tile, jax.ShapeDtypeStruct((m, n), F32), (a, w, resid)
    if emit_normed:
        in_specs.append(pl.BlockSpec((None, 1, bn), lambda i, j: (next_layer, 0, j)))
        out_specs = [tile, tile, pl.BlockSpec((bm, 1), lambda i, j: (i, 0))]
        out_shape = [out_shape, jax.ShapeDtypeStruct((m, n), BF16), jax.ShapeDtypeStruct((m, 1), F32)]
        operands += (next_gains,)
    out = pl.pallas_call(
        functools.partial(_residual_body, scale=scale, emit_normed=emit_normed),
        grid=(m // bm, n // bn),
        in_specs=in_specs,
        out_specs=out_specs,
        out_shape=out_shape,
        compiler_params=_params(2, sequential_last=emit_normed),
        name="residual_proj",
    )(*operands)
    return (out[0], (out[1], out[2])) if emit_normed else (out, None)


def _rope_tables(seq):
    def tables(dim):
        inv = 1.0 / (ROPE_THETA ** (jnp.arange(0, dim, 2, dtype=F32) / dim))
        ang = jnp.arange(seq, dtype=F32)[:, None] * inv[None, :]
        cos, sin = jnp.cos(ang), jnp.sin(ang)
        reps = HEAD_DIM // dim
        return (jnp.tile(jnp.concatenate([cos, cos], axis=-1), (1, reps)),
                jnp.tile(jnp.concatenate([-sin, sin], axis=-1), (1, reps)))
    return tables(HEAD_DIM) + tables(DIFF_QK_DIM)


def _qkv_body(h_ref, ssq_ref, w_ref, cos_a_ref, sin_a_ref, cos_c_ref, sin_c_ref, o_ref, *, segments,
              heads_per_block):
    h = h_ref[...]
    acc = jnp.dot(h, w_ref[...], preferred_element_type=F32) * _inv_rms(ssq_ref, h.shape[1])
    j = pl.program_id(1)

    def partner_a(x):
        return pltpu.roll(x, HEAD_DIM // 2, axis=1)

    def partner_c(x):
        lane = lax.broadcasted_iota(jnp.int32, x.shape, 1)
        first_half = (lane % DIFF_QK_DIM) < (DIFF_QK_DIM // 2)
        return jnp.where(first_half,
                         pltpu.roll(x, HEAD_DIM - DIFF_QK_DIM // 2, axis=1),
                         pltpu.roll(x, DIFF_QK_DIM // 2, axis=1))

    def emit(rope, scale):
        for s in range(heads_per_block):
            cols = slice(s * HEAD_DIM, (s + 1) * HEAD_DIM)
            x = acc[:, cols]
            if rope == "a":
                x = x * cos_a_ref[...] + partner_a(x) * sin_a_ref[...]
            elif rope == "c":
                x = x * cos_c_ref[...] + partner_c(x) * sin_c_ref[...]
            if scale != 1.0:
                x = x * scale
            o_ref[:, cols] = x.astype(o_ref.dtype)

    for start, stop, rope, scale in segments:
        pl.when((j >= start) & (j < stop))(functools.partial(emit, rope, scale))


def _qkv_proj(normed, w_qkv, layer, tables, seq, *, bm=1024, bn=512):
    h, ssq = normed
    m, k = h.shape
    n = w_qkv.shape[2]
    bm = min(bm, seq)
    hpb = bn // HEAD_DIM
    scale_a = LOG2_E / math.sqrt(HEAD_DIM)
    scale_c = LOG2_E / math.sqrt(DIFF_QK_DIM)
    layout = [(QA_OFF, "a", scale_a), (KA_OFF, "a", 1.0), (VA_OFF, None, 1.0),
              (QB_OFF, None, scale_a), (KB_OFF, None, 1.0),
              (QC_OFF, "c", scale_c), (KC_OFF, "c", 1.0), (VC_OFF, None, 1.0)]
    bounds = [off for off, _, _ in layout] + [QKV_HEADS]
    assert all(b % hpb == 0 for b in bounds)
    segments = tuple((bounds[t] // hpb, bounds[t + 1] // hpb, rope, scale)
                     for t, (_, rope, scale) in enumerate(layout))
    pos_blocks = seq // bm
    table_spec = pl.BlockSpec((bm, LANES), lambda i, j: (i % pos_blocks, 0))
    return pl.pallas_call(
        functools.partial(_qkv_body, segments=segments, heads_per_block=hpb),
        grid=(m // bm, n // bn),
        in_specs=[pl.BlockSpec((bm, k), lambda i, j: (i, 0)),
                  pl.BlockSpec((bm, 1), lambda i, j: (i, 0)),
                  pl.BlockSpec((None, k, bn), lambda i, j: (layer, 0, j)),
                  table_spec, table_spec, table_spec, table_spec],
        out_specs=pl.BlockSpec((bm, bn), lambda i, j: (i, j)),
        out_shape=jax.ShapeDtypeStruct((m, n), BF16),
        compiler_params=_params(2),
        name="qkv_proj",
    )(h, ssq, w_qkv, *tables)


def _swa_body(sinks_ref, q_ref, kp_ref, kc_ref, vp_ref, vc_ref, o_ref):
    i = pl.program_id(1)
    blk = WINDOW
    rows = SWA_GROUP * blk
    t = lax.broadcasted_iota(jnp.int32, (rows, 2 * blk), 0) % blk
    c = lax.broadcasted_iota(jnp.int32, (rows, 2 * blk), 1)
    valid = (c > t) & (c <= t + blk) & ((c >= blk) | (i > 0))
    group = lax.broadcasted_iota(jnp.int32, (rows, 1), 0) // blk

    def head_cols(h):
        return slice(h * HEAD_DIM, (h + 1) * HEAD_DIM)

    scores = []
    for hk in range(SWA_KV_HEADS):
        qs = jnp.concatenate([q_ref[:, head_cols(hk * SWA_GROUP + g)] for g in range(SWA_GROUP)], axis=0)
        k = jnp.concatenate([kp_ref[:, head_cols(hk)], kc_ref[:, head_cols(hk)]], axis=0)
        scores.append(lax.dot_general(qs, k, (((1,), (1,)), ((), ())), preferred_element_type=F32))
    for hk in range(SWA_KV_HEADS):
        s = jnp.where(valid, scores[hk], MASK_VALUE)
        sink = jnp.zeros((rows, 1), F32)
        for g in range(SWA_GROUP):
            sink = jnp.where(group == g, sinks_ref[hk * SWA_GROUP + g] * LOG2_E, sink)
        m = jnp.maximum(jnp.max(s, axis=-1, keepdims=True), sink)
        p = jnp.exp2(s - m)
        denom = jnp.sum(p, axis=-1, keepdims=True) + jnp.exp2(sink - m)
        v = jnp.concatenate([vp_ref[:, head_cols(hk)], vc_ref[:, head_cols(hk)]], axis=0)
        o = jnp.dot(p.astype(BF16), v, preferred_element_type=F32) / denom
        for g in range(SWA_GROUP):
            o_ref[:, head_cols(hk * SWA_GROUP + g)] = o[g * blk:(g + 1) * blk].astype(o_ref.dtype)


def _swa_attention(qkv, sinks, batch, seq):
    m = qkv.shape[0]
    blk = WINDOW
    nq = seq // blk
    qw = SWA_Q_HEADS * HEAD_DIM
    kw = SWA_KV_HEADS * HEAD_DIM
    assert QA_OFF % SWA_Q_HEADS == 0 and KA_OFF % SWA_KV_HEADS == 0 and VA_OFF % SWA_KV_HEADS == 0

    def cur(off):
        return pl.BlockSpec((blk, kw), lambda b, i: (b * nq + i, off // SWA_KV_HEADS))

    def prev(off):
        return pl.BlockSpec((blk, kw), lambda b, i: (b * nq + jnp.maximum(i - 1, 0), off // SWA_KV_HEADS))

    return pl.pallas_call(
        _swa_body,
        grid=(batch, nq),
        in_specs=[pl.BlockSpec(memory_space=pltpu.SMEM),
                  pl.BlockSpec((blk, qw), lambda b, i: (b * nq + i, QA_OFF // SWA_Q_HEADS)),
                  prev(KA_OFF), cur(KA_OFF), prev(VA_OFF), cur(VA_OFF)],
        out_specs=pl.BlockSpec((blk, qw), lambda b, i: (b * nq + i, 0)),
        out_shape=jax.ShapeDtypeStruct((m, qw), BF16),
        compiler_params=_params(2),
        name="swa_attention",
    )(sinks, qkv, qkv, qkv, qkv, qkv)


def _stickbreak_body(q_ref, k_ref, v_ref, o_ref, vt_ref, acc_ref, *, blk, heads):
    i = pl.program_id(2)
    nkb = k_ref.shape[0] // blk
    head_cols = [slice(c * HEAD_DIM, (c + 1) * HEAD_DIM) for c in range(heads)]

    @pl.when(i == 0)
    def _():
        for c in range(heads):
            for kb in range(nkb):
                vt_ref[c, kb] = v_ref[kb * blk:(kb + 1) * blk, head_cols[c]].T

    row = lax.broadcasted_iota(jnp.int32, (blk, blk), 0)
    col = lax.broadcasted_iota(jnp.int32, (blk, blk), 1)
    from_key = jnp.where(col >= row, 1.0, 0.0).astype(BF16)
    causal = row < col

    def tile(kb, run, diagonal):
        rows = pl.ds(pl.multiple_of(kb * blk, blk), blk)
        z = [lax.dot_general(k_ref[rows, cols], q_ref[:, cols], (((1,), (1,)), ((), ())),
                             preferred_element_type=F32) for cols in head_cols]
        suffix = []
        for c in range(heads):
            softplus = jnp.maximum(z[c], 0.0) + jnp.log2(1.0 + jnp.exp2(-jnp.abs(z[c])))
            if diagonal:
                softplus = jnp.where(causal, softplus, 0.0)
            hi = softplus.astype(BF16)
            lo = (softplus - hi.astype(F32)).astype(BF16)
            suffix.append(jnp.dot(from_key, hi, preferred_element_type=F32)
                          + jnp.dot(from_key, lo, preferred_element_type=F32))
        new_run = []
        for c in range(heads):
            a = jnp.exp2(z[c] - (run[c] + suffix[c]))
            if diagonal:
                a = jnp.where(causal, a, 0.0)
            pv = jnp.dot(vt_ref[c, kb], a.astype(BF16), preferred_element_type=F32)
            acc_ref[c] = pv if diagonal else acc_ref[c] + pv
            new_run.append(run[c] + suffix[c][0:1])
        return tuple(new_run)

    run = tile(i, tuple(jnp.zeros((1, blk), F32) for _ in range(heads)), True)
    lax.fori_loop(0, i, lambda t, r: tile(i - 1 - t, r, False), run)
    for c in range(heads):
        o_ref[:, head_cols[c]] = acc_ref[c].T.astype(o_ref.dtype)


def _stickbreak_attention(qkv, batch, seq, *, blk=256, heads=8):
    m = qkv.shape[0]
    blk = min(blk, seq)
    nq = seq // blk
    width = heads * HEAD_DIM
    assert all(off % heads == 0 for off in (QB_OFF, KB_OFF, VB_OFF, SB_HEADS))
    return pl.pallas_call(
        functools.partial(_stickbreak_body, blk=blk, heads=heads),
        grid=(batch, SB_HEADS // heads, nq),
        in_specs=[pl.BlockSpec((blk, width), lambda b, h, i: (b * nq + i, QB_OFF // heads + h)),
                  _resident_spec((seq, width), lambda b, h, i: (b, KB_OFF // heads + h)),
                  _resident_spec((seq, width), lambda b, h, i: (b, VB_OFF // heads + h))],
        out_specs=pl.BlockSpec((blk, width), lambda b, h, i: (b * nq + i, h)),
        out_shape=jax.ShapeDtypeStruct((m, SB_HEADS * HEAD_DIM), BF16),
        scratch_shapes=[pltpu.VMEM((heads, nq, HEAD_DIM, blk), BF16),
                        pltpu.VMEM((heads, HEAD_DIM, blk), F32)],
        compiler_params=_params(3, sequential_last=True),
        name="stickbreak_attention",
    )(qkv, qkv, qkv)


def _diff_body(q_ref, k_ref, v_ref, lam_ref, g_ref, o_ref, vt_ref, acc_ref, *, blk, heads, lam_init):
    i = pl.program_id(2)
    nkb = k_ref.shape[0] // blk
    head_cols = [slice(h * HEAD_DIM, (h + 1) * HEAD_DIM) for h in range(heads)]

    @pl.when(i == 0)
    def _():
        for h in range(heads):
            for kb in range(nkb):
                vt_ref[h, kb] = v_ref[kb * blk:(kb + 1) * blk, head_cols[h]].T

    lane = lax.broadcasted_iota(jnp.int32, (blk, HEAD_DIM), 1)
    key = lax.broadcasted_iota(jnp.int32, (blk, 2 * blk), 0)
    qry = lax.broadcasted_iota(jnp.int32, (blk, 2 * blk), 1) % blk
    causal = key <= qry

    def both_maps(q):
        zero = jnp.zeros_like(q)
        return jnp.concatenate([jnp.where(lane < DIFF_QK_DIM, q, zero),
                                jnp.where(lane >= DIFF_QK_DIM, q, zero)], axis=0)

    qq = [both_maps(q_ref[:, cols]) for cols in head_cols]

    def tile(kb, state, diagonal):
        rows = pl.ds(pl.multiple_of(kb * blk, blk), blk)
        scores = [lax.dot_general(k_ref[rows, head_cols[h]], qq[h], (((1,), (1,)), ((), ())),
                                  preferred_element_type=F32) for h in range(heads)]
        new_state = []
        for h in range(heads):
            m, l = state[h]
            s = jnp.where(causal, scores[h], MASK_VALUE) if diagonal else scores[h]
            m_new = jnp.maximum(m, jnp.max(s, axis=0, keepdims=True))
            alpha = jnp.exp2(m - m_new)
            p = jnp.exp2(s - m_new)
            pv = jnp.dot(vt_ref[h, kb], p.astype(BF16), preferred_element_type=F32)
            acc_ref[h] = pv if diagonal else alpha * acc_ref[h] + pv
            new_state.append((m_new, alpha * l + jnp.sum(p, axis=0, keepdims=True)))
        return tuple(new_state)

    state = tuple((jnp.full((1, 2 * blk), MASK_VALUE, F32), jnp.zeros((1, 2 * blk), F32))
                  for _ in range(heads))
    state = tile(i, state, True)
    state = lax.fori_loop(0, i, lambda kb, s: tile(kb, s, False), state)

    lam_vecs = lam_ref[...]
    lam = (jnp.exp(jnp.sum(lam_vecs[0:1] * lam_vecs[1:2], axis=-1, keepdims=True))
           - jnp.exp(jnp.sum(lam_vecs[2:3] * lam_vecs[3:4], axis=-1, keepdims=True))
           + lam_init)
    for h in range(heads):
        o = acc_ref[h] / state[h][1]
        d = o[:, :blk] - lam * o[:, blk:]
        ms = jnp.mean(d * d, axis=0, keepdims=True)
        y = d * lax.rsqrt(ms + DIFF_NORM_EPS) * g_ref[...]
        o_ref[:, head_cols[h]] = (y * (1.0 - lam_init)).T.astype(o_ref.dtype)


def _diff_attention(qkv, lam_vecs, diff_norm, layer, lam_init, batch, seq, *, blk=512, heads=4):
    m = qkv.shape[0]
    blk = min(blk, seq)
    nq = seq // blk
    width = heads * HEAD_DIM
    assert all(off % heads == 0 for off in (QC_OFF, KC_OFF, VC_OFF, DIFF_HEADS))
    return pl.pallas_call(
        functools.partial(_diff_body, blk=blk, heads=heads, lam_init=lam_init),
        grid=(batch, DIFF_HEADS // heads, nq),
        in_specs=[pl.BlockSpec((blk, width), lambda b, h, i: (b * nq + i, QC_OFF // heads + h)),
                  _resident_spec((seq, width), lambda b, h, i: (b, KC_OFF // heads + h)),
                  _resident_spec((seq, width), lambda b, h, i: (b, VC_OFF // heads + h)),
                  pl.BlockSpec(lam_vecs.shape, lambda b, h, i: (0, 0)),
                  pl.BlockSpec((None, HEAD_DIM, 1), lambda b, h, i: (layer, 0, 0))],
        out_specs=pl.BlockSpec((blk, width), lambda b, h, i: (b * nq + i, h)),
        out_shape=jax.ShapeDtypeStruct((m, DIFF_HEADS * HEAD_DIM), BF16),
        scratch_shapes=[pltpu.VMEM((heads, nq, HEAD_DIM, blk), BF16),
                        pltpu.VMEM((heads, HEAD_DIM, 2 * blk), F32)],
        compiler_params=_params(3, sequential_last=True),
        name="diff_attention",
    )(qkv, qkv, qkv, lam_vecs, diff_norm)


def _merge_body(h_ref, ssq_ref, wga_ref, wgb_ref, wgc_ref, oa_ref, wa_ref, ob_ref, wb_ref, oc_ref, wc_ref, o_ref):
    h = h_ref[...]
    inv = _inv_rms(ssq_ref, h.shape[1])

    def gated(wg_ref, x_ref, w_ref):
        gate = jax.nn.sigmoid(jnp.dot(h, wg_ref[...], preferred_element_type=F32) * inv)
        return gate * jnp.dot(x_ref[...], w_ref[...], preferred_element_type=F32)

    merged = gated(wga_ref, oa_ref, wa_ref) + gated(wgb_ref, ob_ref, wb_ref) + gated(wgc_ref, oc_ref, wc_ref)
    o_ref[...] = merged.astype(o_ref.dtype)


def _gated_merge(normed, w_gate, out_a, w_a, out_b, w_b, out_c, w_c, layer, *, bm=512, bn=256):
    h, ssq = normed
    m, k = h.shape
    d = w_a.shape[2]
    bm, bn = min(bm, m), min(bn, d)
    nj = d // bn

    def act(x):
        return pl.BlockSpec((bm, x.shape[1]), lambda i, j: (i, 0))

    def weight(w, block_off=0):
        return pl.BlockSpec((None, w.shape[1], bn), lambda i, j: (layer, 0, j + block_off))

    return pl.pallas_call(
        _merge_body,
        grid=(m // bm, nj),
        in_specs=[act(h), act(ssq), weight(w_gate), weight(w_gate, nj), weight(w_gate, 2 * nj),
                  act(out_a), weight(w_a), act(out_b), weight(w_b), act(out_c), weight(w_c)],
        out_specs=pl.BlockSpec((bm, bn), lambda i, j: (i, j)),
        out_shape=jax.ShapeDtypeStruct((m, d), BF16),
        compiler_params=_params(2),
        name="gated_merge",
    )(h, ssq, w_gate, w_gate, w_gate, out_a, w_a, out_b, w_b, out_c, w_c)


def kernel(x, ffn1_norm, ffn1_w_in, ffn1_w_out, mix_norm, w_qkv, w_gate, sinks, lambda_q1, lambda_k1, lambda_q2, lambda_k2, diff_norm, w_branch_a, w_branch_b, w_branch_c, w_out, ffn2_norm, ffn2_w_in, ffn2_w_out, final_norm):
    batch, seq, d_model = x.shape
    depth = ffn1_w_in.shape[0]
    assert w_qkv.shape[2] == QKV_HEADS * HEAD_DIM
    m = batch * seq

    def weights(w):
        return w.astype(BF16)

    def gains(g):
        return g.astype(F32).reshape(g.shape[0], 1, g.shape[1])

    ffn1_w_in, ffn1_w_out, ffn2_w_in, ffn2_w_out = map(weights, (ffn1_w_in, ffn1_w_out, ffn2_w_in, ffn2_w_out))
    w_qkv, w_gate, w_out = map(weights, (w_qkv, w_gate, w_out))
    w_branch_a, w_branch_b, w_branch_c = map(weights, (w_branch_a, w_branch_b, w_branch_c))
    ffn1_norm, mix_norm, ffn2_norm = map(gains, (ffn1_norm, mix_norm, ffn2_norm))
    diff_norm = diff_norm.astype(F32).reshape(depth, HEAD_DIM, 1)
    final_gain = final_norm.astype(F32).reshape(1, d_model)
    tables = _rope_tables(seq)
    sinks = sinks.astype(F32)
    lam_all = jnp.stack([lambda_q1, lambda_k1, lambda_q2, lambda_k2], axis=1).astype(F32)

    x = x.astype(F32).reshape(m, d_model)
    normed = _normed_operand(x, ffn1_norm, 0)
    for layer in range(depth):
        hidden = _swiglu_up(normed, ffn1_w_in, layer)
        x, normed = _residual_proj(hidden, ffn1_w_out, x, layer, 0.5, mix_norm, layer)

        qkv = _qkv_proj(normed, w_qkv, layer, tables, seq)
        out_a = _swa_attention(qkv, sinks[layer], batch, seq)
        out_b = _stickbreak_attention(qkv, batch, seq)
        lam_init = 0.8 - 0.6 * math.exp(-0.3 * layer)
        out_c = _diff_attention(qkv, lam_all[layer], diff_norm, layer, lam_init, batch, seq)
        merged = _gated_merge(normed, w_gate, out_a, w_branch_a, out_b, w_branch_b, out_c, w_branch_c, layer)
        x, normed = _residual_proj(merged, w_out, x, layer, 1.0, ffn2_norm, layer)

        hidden = _swiglu_up(normed, ffn2_w_in, layer)
        if layer + 1 < depth:
            x, normed = _residual_proj(hidden, ffn2_w_out, x, layer, 0.5, ffn1_norm, layer + 1)
        else:
            x, _ = _residual_proj(hidden, ffn2_w_out, x, layer, 0.5)

    return _rmsnorm(x, final_gain).reshape(batch, seq, d_model)
```

```python
import functools
import math

import jax
import jax.numpy as jnp
from jax import lax
from jax.experimental import pallas as pl
from jax.experimental.pallas import tpu as pltpu

F32 = jnp.float32
BF16 = jnp.bfloat16

HEAD_DIM = 128
ROPE_THETA = 10000.0
EPS = 1e-6
SWA_Q_HEADS = 16
SWA_KV_HEADS = 4
SWA_GROUP = SWA_Q_HEADS // SWA_KV_HEADS
WINDOW = 128
SB_HEADS = 8
DIFF_HEADS = 8
DIFF_QK_DIM = HEAD_DIM // 2
DIFF_NORM_EPS = 1e-5
QA_OFF = 0
KA_OFF = QA_OFF + SWA_Q_HEADS
VA_OFF = KA_OFF + SWA_KV_HEADS
QB_OFF = VA_OFF + SWA_KV_HEADS
KB_OFF = QB_OFF + SB_HEADS
VB_OFF = KB_OFF + SB_HEADS
QC_OFF = VB_OFF + SB_HEADS
KC_OFF = QC_OFF + DIFF_HEADS
VC_OFF = KC_OFF + DIFF_HEADS
QKV_HEADS = VC_OFF + DIFF_HEADS

V7X_VMEM_BYTES = 64 * 1024 * 1024
VMEM_LIMIT_BYTES = V7X_VMEM_BYTES * 3 // 4
LANES = 128
BF16_SUBLANES = 16
MASK_VALUE = -1e30
LOG2_E = math.log2(math.e)


def _params(n_grid_dims, sequential=0):
    semantics = ("parallel",) * (n_grid_dims - sequential) + ("arbitrary",) * sequential
    return pltpu.CompilerParams(dimension_semantics=semantics, vmem_limit_bytes=VMEM_LIMIT_BYTES)


def _resident_spec(block_shape, index_map):
    return pl.BlockSpec(block_shape, index_map, pipeline_mode=pl.Buffered(1))


def _inv_rms(ssq_ref, width, eps=EPS):
    return lax.rsqrt(ssq_ref[...] * (1.0 / width) + eps)


def _rmsnorm_body(x_ref, g_ref, o_ref, *, eps):
    x = x_ref[...]
    ms = jnp.mean(x * x, axis=-1, keepdims=True)
    o_ref[...] = (x * lax.rsqrt(ms + eps) * g_ref[...]).astype(o_ref.dtype)


def _rmsnorm(x, gain, *, block_rows=256):
    m, d = x.shape
    br = min(block_rows, m)
    return pl.pallas_call(
        functools.partial(_rmsnorm_body, eps=EPS),
        grid=(m // br,),
        in_specs=[pl.BlockSpec((br, d), lambda i: (i, 0)),
                  pl.BlockSpec((1, d), lambda i: (0, 0))],
        out_specs=pl.BlockSpec((br, d), lambda i: (i, 0)),
        out_shape=jax.ShapeDtypeStruct((m, d), F32),
        compiler_params=_params(1),
        name="rmsnorm",
    )(x, gain)


def _normed_operand_body(x_ref, g_ref, xg_ref, ssq_ref):
    x = x_ref[...]
    xg_ref[...] = (x * g_ref[...]).astype(xg_ref.dtype)
    ssq_ref[...] = jnp.sum(x * x, axis=-1, keepdims=True)


def _normed_operand(x, gains, layer, *, block_rows=256):
    m, d = x.shape
    br = min(block_rows, m)
    return pl.pallas_call(
        _normed_operand_body,
        grid=(m // br,),
        in_specs=[pl.BlockSpec((br, d), lambda i: (i, 0)),
                  pl.BlockSpec((None, 1, d), lambda i: (layer, 0, 0))],
        out_specs=[pl.BlockSpec((br, d), lambda i: (i, 0)),
                   pl.BlockSpec((br, 1), lambda i: (i, 0))],
        out_shape=[jax.ShapeDtypeStruct((m, d), BF16), jax.ShapeDtypeStruct((m, 1), F32)],
        compiler_params=_params(1),
        name="normed_operand",
    )(x, gains)


def _dense_call(body, grid, in_specs, out_specs, out_shapes, operands, guests, name, sequential=0):
    steps = grid[0] * grid[1]
    n_in, n_out, n_guests = len(in_specs), len(out_specs), len(guests)
    in_specs, out_specs, out_shapes, operands = list(in_specs), list(out_specs), list(out_shapes), list(operands)
    for weights, layer in guests:
        _, k, n = weights.shape
        rows = max(BF16_SUBLANES, pl.next_power_of_2(pl.cdiv(k, steps)))
        assert k % rows == 0
        last = k // rows - 1

        def block(i, j, last=last):
            return jnp.minimum(i * grid[1] + j, last)

        in_specs.append(pl.BlockSpec((None, rows, n),
                                     lambda i, j, layer=layer, block=block: (layer, block(i, j), 0)))
        out_specs.append(pl.BlockSpec((rows, n), lambda i, j, block=block: (block(i, j), 0)))
        out_shapes.append(jax.ShapeDtypeStruct((k, n), BF16))
        operands.append(weights)

    def full_body(*refs):
        ins, guest_ins = refs[:n_in], refs[n_in:n_in + n_guests]
        outs = refs[n_in + n_guests:n_in + n_guests + n_out]
        guest_outs = refs[n_in + n_guests + n_out:]
        for src, dst in zip(guest_ins, guest_outs):
            dst[...] = src[...].astype(dst.dtype)
        body(ins, outs)

    results = pl.pallas_call(
        full_body,
        grid=grid,
        in_specs=in_specs,
        out_specs=out_specs,
        out_shape=out_shapes,
        compiler_params=_params(2, sequential=2 if guests else sequential),
        name=name,
    )(*operands)
    return results[:n_out], results[n_out:]


def _swiglu_body(in_refs, out_refs):
    h_ref, ssq_ref, wg_ref, wu_ref = in_refs
    (o_ref,) = out_refs
    h = h_ref[...]
    inv = _inv_rms(ssq_ref, h.shape[1])
    g = jnp.dot(h, wg_ref[...], preferred_element_type=F32) * inv
    u = jnp.dot(h, wu_ref[...], preferred_element_type=F32) * inv
    o_ref[...] = (g * jax.nn.sigmoid(g) * u).astype(o_ref.dtype)


def _swiglu_up(normed, w_in, guests=(), *, bm=1024, bn=512):
    h, ssq = normed
    m, k = h.shape
    f = w_in.shape[1] // 2
    bm, bn = min(bm, m), min(bn, f)
    nj = f // bn
    (hidden,), cast = _dense_call(
        _swiglu_body, (m // bm, nj),
        [pl.BlockSpec((bm, k), lambda i, j: (i, 0)),
         pl.BlockSpec((bm, 1), lambda i, j: (i, 0)),
         pl.BlockSpec((k, bn), lambda i, j: (0, j)),
         pl.BlockSpec((k, bn), lambda i, j: (0, j + nj))],
        [pl.BlockSpec((bm, bn), lambda i, j: (i, j))],
        [jax.ShapeDtypeStruct((m, f), BF16)],
        (h, ssq, w_in, w_in), guests, "swiglu_up")
    return hidden, cast


def _residual_body(in_refs, out_refs, *, scale):
    a_ref, w_ref, r_ref = in_refs[:3]
    x = r_ref[...] + scale * jnp.dot(a_ref[...], w_ref[...], preferred_element_type=F32)
    out_refs[0][...] = x
    if len(out_refs) == 1:
        return
    g_ref = in_refs[3]
    _, xg_ref, ssq_ref = out_refs
    xg_ref[...] = (x * g_ref[...]).astype(xg_ref.dtype)

    @pl.when(pl.program_id(1) == 0)
    def _():
        ssq_ref[...] = jnp.zeros_like(ssq_ref)

    ssq_ref[...] += jnp.sum(x * x, axis=-1, keepdims=True)


def _residual_proj(a, w, resid, scale, next_gains=None, next_layer=None, guests=(), *, bm=1024, bn=512):
    m, k = a.shape
    n = w.shape[1]
    bm, bn = min(bm, m), min(bn, n)
    emit_normed = next_gains is not None
    tile = pl.BlockSpec((bm, bn), lambda i, j: (i, j))
    in_specs = [pl.BlockSpec((bm, k), lambda i, j: (i, 0)), pl.BlockSpec((k, bn), lambda i, j: (0, j)), tile]
    out_specs, out_shapes, operands = [tile], [jax.ShapeDtypeStruct((m, n), F32)], (a, w, resid)
    if emit_normed:
        in_specs.append(pl.BlockSpec((None, 1, bn), lambda i, j: (next_layer, 0, j)))
        out_specs += [tile, pl.BlockSpec((bm, 1), lambda i, j: (i, 0))]
        out_shapes += [jax.ShapeDtypeStruct((m, n), BF16), jax.ShapeDtypeStruct((m, 1), F32)]
        operands += (next_gains,)
    out, cast = _dense_call(functools.partial(_residual_body, scale=scale), (m // bm, n // bn), in_specs,
                            out_specs, out_shapes, operands, guests, "residual_proj",
                            sequential=int(emit_normed))
    return out[0], ((out[1], out[2]) if emit_normed else None), cast


_SCALE_FULL = LOG2_E / math.sqrt(HEAD_DIM)
_SCALE_HALF = LOG2_E / math.sqrt(DIFF_QK_DIM)
_QKV_AB_LAYOUT = ((QA_OFF, SWA_Q_HEADS, HEAD_DIM, _SCALE_FULL), (KA_OFF, SWA_KV_HEADS, HEAD_DIM, 1.0),
                  (VA_OFF, SWA_KV_HEADS, None, 1.0), (QB_OFF, SB_HEADS, None, _SCALE_FULL),
                  (KB_OFF, 2 * SB_HEADS, None, 1.0))
_QKV_C_LAYOUT = ((QC_OFF, DIFF_HEADS, DIFF_QK_DIM, _SCALE_HALF), (KC_OFF, DIFF_HEADS, DIFF_QK_DIM, 1.0),
                 (VC_OFF, DIFF_HEADS, None, 1.0))


def _epilogue_tables(seq, layout):
    mult, rot = [], []
    for _, _, rope_dim, scale in layout:
        if rope_dim is None:
            mult.append(jnp.full((seq, HEAD_DIM), scale, F32))
            rot.append(jnp.zeros((seq, HEAD_DIM), F32))
            continue
        inv = 1.0 / (ROPE_THETA ** (jnp.arange(0, rope_dim, 2, dtype=F32) / rope_dim))
        ang = jnp.arange(seq, dtype=F32)[:, None] * inv[None, :]
        cos, sin = jnp.cos(ang), jnp.sin(ang)
        reps = HEAD_DIM // rope_dim
        mult.append(jnp.tile(jnp.concatenate([cos, cos], axis=-1), (1, reps)) * scale)
        rot.append(jnp.tile(jnp.concatenate([-sin, sin], axis=-1), (1, reps)) * scale)
    return jnp.stack(mult), jnp.stack(rot)


def _qkv_body(in_refs, out_refs, *, rope_dim, heads_per_block):
    h_ref, ssq_ref, w_ref, mult_ref, rot_ref = in_refs
    (o_ref,) = out_refs
    h = h_ref[...]
    acc = jnp.dot(h, w_ref[...], preferred_element_type=F32) * _inv_rms(ssq_ref, h.shape[1])

    def partner(x):
        if rope_dim == HEAD_DIM:
            return pltpu.roll(x, HEAD_DIM // 2, axis=1)
        lane = lax.broadcasted_iota(jnp.int32, x.shape, 1)
        return jnp.where((lane % rope_dim) < (rope_dim // 2),
                         pltpu.roll(x, HEAD_DIM - rope_dim // 2, axis=1),
                         pltpu.roll(x, rope_dim // 2, axis=1))

    for s in range(heads_per_block):
        cols = slice(s * HEAD_DIM, (s + 1) * HEAD_DIM)
        x = acc[:, cols]
        o_ref[:, cols] = (x * mult_ref[...] + partner(x) * rot_ref[...]).astype(o_ref.dtype)


def _qkv_proj(normed, w_qkv, layout, tables, seq, guests=(), *, bm=1024, max_bn=1024):
    h, ssq = normed
    mult, rot = tables
    m, k = h.shape
    bm = min(bm, seq)
    hpb = math.gcd(max_bn // HEAD_DIM, *(count for _, count, _, _ in layout))
    bn = hpb * HEAD_DIM
    first_head = layout[0][0]
    n_heads = sum(count for _, count, _, _ in layout)
    rope_dims = {rope_dim for _, _, rope_dim, _ in layout if rope_dim is not None}
    assert len(rope_dims) == 1 and first_head % hpb == 0
    block_segment = []
    for seg, (_, count, _, _) in enumerate(layout):
        assert count % hpb == 0
        block_segment += [seg] * (count // hpb)

    def segment_of(j):
        return sum(jnp.where(j == blk, seg, 0) for blk, seg in enumerate(block_segment) if seg)

    pos_blocks = seq // bm
    table_spec = pl.BlockSpec((None, bm, LANES), lambda i, j: (segment_of(j), i % pos_blocks, 0))
    first_block = first_head // hpb
    (qkv,), cast = _dense_call(
        functools.partial(_qkv_body, rope_dim=rope_dims.pop(), heads_per_block=hpb),
        (m // bm, n_heads // hpb),
        [pl.BlockSpec((bm, k), lambda i, j: (i, 0)),
         pl.BlockSpec((bm, 1), lambda i, j: (i, 0)),
         pl.BlockSpec((k, bn), lambda i, j: (0, first_block + j)),
         table_spec, table_spec],
        [pl.BlockSpec((bm, bn), lambda i, j: (i, j))],
        [jax.ShapeDtypeStruct((m, n_heads * HEAD_DIM), BF16)],
        (h, ssq, w_qkv, mult, rot), guests, "qkv_proj")
    return qkv, cast


def _swa_body(sinks_ref, q_ref, kp_ref, kc_ref, vp_ref, vc_ref, o_ref):
    i = pl.program_id(1)
    blk = WINDOW
    rows = SWA_GROUP * blk
    t = lax.broadcasted_iota(jnp.int32, (rows, 2 * blk), 0) % blk
    c = lax.broadcasted_iota(jnp.int32, (rows, 2 * blk), 1)
    valid = (c > t) & (c <= t + blk) & ((c >= blk) | (i > 0))
    group = lax.broadcasted_iota(jnp.int32, (rows, 1), 0) // blk

    def head_cols(h):
        return slice(h * HEAD_DIM, (h + 1) * HEAD_DIM)

    scores = []
    for hk in range(SWA_KV_HEADS):
        qs = jnp.concatenate([q_ref[:, head_cols(hk * SWA_GROUP + g)] for g in range(SWA_GROUP)], axis=0)
        k = jnp.concatenate([kp_ref[:, head_cols(hk)], kc_ref[:, head_cols(hk)]], axis=0)
        scores.append(lax.dot_general(qs, k, (((1,), (1,)), ((), ())), preferred_element_type=F32))
    for hk in range(SWA_KV_HEADS):
        s = jnp.where(valid, scores[hk], MASK_VALUE)
        sink = jnp.zeros((rows, 1), F32)
        for g in range(SWA_GROUP):
            sink = jnp.where(group == g, sinks_ref[hk * SWA_GROUP + g] * LOG2_E, sink)
        m = jnp.maximum(jnp.max(s, axis=-1, keepdims=True), sink)
        p = jnp.exp2(s - m)
        denom = jnp.sum(p, axis=-1, keepdims=True) + jnp.exp2(sink - m)
        v = jnp.concatenate([vp_ref[:, head_cols(hk)], vc_ref[:, head_cols(hk)]], axis=0)
        o = jnp.dot(p.astype(BF16), v, preferred_element_type=F32) / denom
        for g in range(SWA_GROUP):
            o_ref[:, head_cols(hk * SWA_GROUP + g)] = o[g * blk:(g + 1) * blk].astype(o_ref.dtype)


def _swa_attention(qkv, sinks, batch, seq):
    m = qkv.shape[0]
    blk = WINDOW
    nq = seq // blk
    qw = SWA_Q_HEADS * HEAD_DIM
    kw = SWA_KV_HEADS * HEAD_DIM
    assert QA_OFF % SWA_Q_HEADS == 0 and KA_OFF % SWA_KV_HEADS == 0 and VA_OFF % SWA_KV_HEADS == 0

    def cur(off):
        return pl.BlockSpec((blk, kw), lambda b, i: (b * nq + i, off // SWA_KV_HEADS))

    def prev(off):
        return pl.BlockSpec((blk, kw), lambda b, i: (b * nq + jnp.maximum(i - 1, 0), off // SWA_KV_HEADS))

    return pl.pallas_call(
        _swa_body,
        grid=(batch, nq),
        in_specs=[pl.BlockSpec(memory_space=pltpu.SMEM),
                  pl.BlockSpec((blk, qw), lambda b, i: (b * nq + i, QA_OFF // SWA_Q_HEADS)),
                  prev(KA_OFF), cur(KA_OFF), prev(VA_OFF), cur(VA_OFF)],
        out_specs=pl.BlockSpec((blk, qw), lambda b, i: (b * nq + i, 0)),
        out_shape=jax.ShapeDtypeStruct((m, qw), BF16),
        compiler_params=_params(2),
        name="swa_attention",
    )(sinks, qkv, qkv, qkv, qkv, qkv)


def _neg_abs(x):
    bits = lax.bitcast_convert_type(x, jnp.int32) | jnp.int32(-2 ** 31)
    return lax.bitcast_convert_type(bits, F32)


def _stickbreak_body(q_ref, k_ref, v_ref, o_ref, vt_ref, acc_ref, *, blk, heads):
    i = pl.program_id(2)
    nkb = k_ref.shape[0] // blk
    head_cols = [slice(c * HEAD_DIM, (c + 1) * HEAD_DIM) for c in range(heads)]

    @pl.when(i == 0)
    def _():
        for c in range(heads):
            for kb in range(nkb):
                vt_ref[c, kb] = v_ref[kb * blk:(kb + 1) * blk, head_cols[c]].T

    row = lax.broadcasted_iota(jnp.int32, (blk, blk), 0)
    col = lax.broadcasted_iota(jnp.int32, (blk, blk), 1)
    from_key = jnp.where(col >= row, 1.0, 0.0).astype(BF16)
    causal = row < col

    def tile(kb, run, diagonal):
        rows = pl.ds(pl.multiple_of(kb * blk, blk), blk)
        z = [lax.dot_general(k_ref[rows, cols], q_ref[:, cols], (((1,), (1,)), ((), ())),
                             preferred_element_type=F32) for cols in head_cols]
        suffix = []
        for c in range(heads):
            softplus = jnp.maximum(z[c], 0.0) + jnp.log2(1.0 + jnp.exp2(_neg_abs(z[c])))
            if diagonal:
                softplus = jnp.where(causal, softplus, 0.0)
            hi = softplus.astype(BF16)
            lo = (softplus - hi.astype(F32)).astype(BF16)
            suffix.append(jnp.dot(from_key, hi, preferred_element_type=F32)
                          + jnp.dot(from_key, lo, preferred_element_type=F32))
        new_run = []
        for c in range(heads):
            a = jnp.exp2(z[c] - (run[c] + suffix[c]))
            if diagonal:
                a = jnp.where(causal, a, 0.0)
            pv = jnp.dot(vt_ref[c, kb], a.astype(BF16), preferred_element_type=F32)
            acc_ref[c] = pv if diagonal else acc_ref[c] + pv
            new_run.append(run[c] + suffix[c][0:1])
        return tuple(new_run)

    run = tile(i, tuple(jnp.zeros((1, blk), F32) for _ in range(heads)), True)
    lax.fori_loop(0, i, lambda t, r: tile(i - 1 - t, r, False), run)
    for c in range(heads):
        o_ref[:, head_cols[c]] = acc_ref[c].T.astype(o_ref.dtype)


def _stickbreak_attention(qkv, batch, seq, *, blk=256, heads=8):
    m = qkv.shape[0]
    blk = min(blk, seq)
    nq = seq // blk
    width = heads * HEAD_DIM
    assert all(off % heads == 0 for off in (QB_OFF, KB_OFF, VB_OFF, SB_HEADS))
    return pl.pallas_call(
        functools.partial(_stickbreak_body, blk=blk, heads=heads),
        grid=(batch, SB_HEADS // heads, nq),
        in_specs=[pl.BlockSpec((blk, width), lambda b, h, i: (b * nq + i, QB_OFF // heads + h)),
                  _resident_spec((seq, width), lambda b, h, i: (b, KB_OFF // heads + h)),
                  _resident_spec((seq, width), lambda b, h, i: (b, VB_OFF // heads + h))],
        out_specs=pl.BlockSpec((blk, width), lambda b, h, i: (b * nq + i, h)),
        out_shape=jax.ShapeDtypeStruct((m, SB_HEADS * HEAD_DIM), BF16),
        scratch_shapes=[pltpu.VMEM((heads, nq, HEAD_DIM, blk), BF16),
                        pltpu.VMEM((heads, HEAD_DIM, blk), F32)],
        compiler_params=_params(3, sequential=1),
        name="stickbreak_attention",
    )(qkv, qkv, qkv)


def _diff_body(q_ref, k_ref, v_ref, lam_ref, g_ref, o_ref, vt_ref, acc_ref, *, blk, heads, lam_init):
    i = pl.program_id(2)
    nkb = k_ref.shape[0] // blk
    head_cols = [slice(h * HEAD_DIM, (h + 1) * HEAD_DIM) for h in range(heads)]

    @pl.when(i == 0)
    def _():
        for h in range(heads):
            for kb in range(nkb):
                vt_ref[h, kb] = v_ref[kb * blk:(kb + 1) * blk, head_cols[h]].T

    lane = lax.broadcasted_iota(jnp.int32, (blk, HEAD_DIM), 1)
    key = lax.broadcasted_iota(jnp.int32, (blk, 2 * blk), 0)
    qry = lax.broadcasted_iota(jnp.int32, (blk, 2 * blk), 1) % blk
    causal = key <= qry

    def both_maps(q):
        zero = jnp.zeros_like(q)
        return jnp.concatenate([jnp.where(lane < DIFF_QK_DIM, q, zero),
                                jnp.where(lane >= DIFF_QK_DIM, q, zero)], axis=0)

    qq = [both_maps(q_ref[:, cols]) for cols in head_cols]

    def tile(kb, state, diagonal):
        rows = pl.ds(pl.multiple_of(kb * blk, blk), blk)
        scores = [lax.dot_general(k_ref[rows, head_cols[h]], qq[h], (((1,), (1,)), ((), ())),
                                  preferred_element_type=F32) for h in range(heads)]
        new_state = []
        for h in range(heads):
            m, l = state[h]
            s = jnp.where(causal, scores[h], MASK_VALUE) if diagonal else scores[h]
            m_new = jnp.maximum(m, jnp.max(s, axis=0, keepdims=True))
            alpha = jnp.exp2(m - m_new)
            p = jnp.exp2(s - m_new)
            pv = jnp.dot(vt_ref[h, kb], p.astype(BF16), preferred_element_type=F32)
            acc_ref[h] = pv if diagonal else alpha * acc_ref[h] + pv
            new_state.append((m_new, alpha * l + jnp.sum(p, axis=0, keepdims=True)))
        return tuple(new_state)

    state = tuple((jnp.full((1, 2 * blk), MASK_VALUE, F32), jnp.zeros((1, 2 * blk), F32))
                  for _ in range(heads))
    state = tile(i, state, True)
    state = lax.fori_loop(0, i, lambda kb, s: tile(kb, s, False), state)

    lam_vecs = lam_ref[...]
    lam = (jnp.exp(jnp.sum(lam_vecs[0:1] * lam_vecs[1:2], axis=-1, keepdims=True))
           - jnp.exp(jnp.sum(lam_vecs[2:3] * lam_vecs[3:4], axis=-1, keepdims=True))
           + lam_init)
    for h in range(heads):
        o = acc_ref[h] / state[h][1]
        d = o[:, :blk] - lam * o[:, blk:]
        ms = jnp.mean(d * d, axis=0, keepdims=True)
        y = d * lax.rsqrt(ms + DIFF_NORM_EPS) * g_ref[...]
        o_ref[:, head_cols[h]] = (y * (1.0 - lam_init)).T.astype(o_ref.dtype)


def _diff_attention(qkv, lam_vecs, diff_norm, layer, lam_init, batch, seq, *, blk=512, heads=4):
    m = qkv.shape[0]
    blk = min(blk, seq)
    nq = seq // blk
    width = heads * HEAD_DIM
    q_blk, k_blk, v_blk = ((off - QC_OFF) // heads for off in (QC_OFF, KC_OFF, VC_OFF))
    assert DIFF_HEADS % heads == 0
    return pl.pallas_call(
        functools.partial(_diff_body, blk=blk, heads=heads, lam_init=lam_init),
        grid=(batch, DIFF_HEADS // heads, nq),
        in_specs=[pl.BlockSpec((blk, width), lambda b, h, i: (b * nq + i, q_blk + h)),
                  _resident_spec((seq, width), lambda b, h, i: (b, k_blk + h)),
                  _resident_spec((seq, width), lambda b, h, i: (b, v_blk + h)),
                  pl.BlockSpec(lam_vecs.shape, lambda b, h, i: (0, 0)),
                  pl.BlockSpec((None, HEAD_DIM, 1), lambda b, h, i: (layer, 0, 0))],
        out_specs=pl.BlockSpec((blk, width), lambda b, h, i: (b * nq + i, h)),
        out_shape=jax.ShapeDtypeStruct((m, DIFF_HEADS * HEAD_DIM), BF16),
        scratch_shapes=[pltpu.VMEM((heads, nq, HEAD_DIM, blk), BF16),
                        pltpu.VMEM((heads, HEAD_DIM, 2 * blk), F32)],
        compiler_params=_params(3, sequential=1),
        name="diff_attention",
    )(qkv, qkv, qkv, lam_vecs, diff_norm)


def _merge_body(in_refs, out_refs):
    h_ref, ssq_ref, wga_ref, wgb_ref, wgc_ref, oa_ref, wa_ref, ob_ref, wb_ref, oc_ref, wc_ref = in_refs
    (o_ref,) = out_refs
    h = h_ref[...]
    inv = _inv_rms(ssq_ref, h.shape[1])

    def gated(wg_ref, x_ref, w_ref):
        gate = jax.nn.sigmoid(jnp.dot(h, wg_ref[...], preferred_element_type=F32) * inv)
        return gate * jnp.dot(x_ref[...], w_ref[...], preferred_element_type=F32)

    merged = gated(wga_ref, oa_ref, wa_ref) + gated(wgb_ref, ob_ref, wb_ref) + gated(wgc_ref, oc_ref, wc_ref)
    o_ref[...] = merged.astype(o_ref.dtype)


def _gated_merge(normed, w_gate, out_a, w_a, out_b, w_b, out_c, w_c, guests=(), *, bm=512, bn=256):
    h, ssq = normed
    m, k = h.shape
    d = w_a.shape[1]
    bm, bn = min(bm, m), min(bn, d)
    nj = d // bn

    def act(x):
        return pl.BlockSpec((bm, x.shape[1]), lambda i, j: (i, 0))

    def weight(w, block_off=0):
        return pl.BlockSpec((w.shape[0], bn), lambda i, j: (0, j + block_off))

    (merged,), cast = _dense_call(
        _merge_body, (m // bm, nj),
        [act(h), act(ssq), weight(w_gate), weight(w_gate, nj), weight(w_gate, 2 * nj),
         act(out_a), weight(w_a), act(out_b), weight(w_b), act(out_c), weight(w_c)],
        [pl.BlockSpec((bm, bn), lambda i, j: (i, j))],
        [jax.ShapeDtypeStruct((m, d), BF16)],
        (h, ssq, w_gate, w_gate, w_gate, out_a, w_a, out_b, w_b, out_c, w_c), guests, "gated_merge")
    return merged, cast


def kernel(x, ffn1_norm, ffn1_w_in, ffn1_w_out, mix_norm, w_qkv, w_gate, sinks, lambda_q1, lambda_k1, lambda_q2, lambda_k2, diff_norm, w_branch_a, w_branch_b, w_branch_c, w_out, ffn2_norm, ffn2_w_in, ffn2_w_out, final_norm):
    batch, seq, d_model = x.shape
    depth = ffn1_w_in.shape[0]
    assert w_qkv.shape[2] == QKV_HEADS * HEAD_DIM
    m = batch * seq

    def gains(g):
        return g.astype(F32).reshape(g.shape[0], 1, g.shape[1])

    f32_weights = dict(ffn1_w_in=ffn1_w_in, ffn1_w_out=ffn1_w_out, w_qkv=w_qkv, w_gate=w_gate, w_out=w_out,
                       w_branch_a=w_branch_a, w_branch_b=w_branch_b, w_branch_c=w_branch_c,
                       ffn2_w_in=ffn2_w_in, ffn2_w_out=ffn2_w_out)
    f32_weights = {name: w.astype(F32) for name, w in f32_weights.items()}
    w = {name: full[0].astype(BF16) for name, full in f32_weights.items()}
    ffn1_norm, mix_norm, ffn2_norm = map(gains, (ffn1_norm, mix_norm, ffn2_norm))
    diff_norm = diff_norm.astype(F32).reshape(depth, HEAD_DIM, 1)
    final_gain = final_norm.astype(F32).reshape(1, d_model)
    tables_ab = _epilogue_tables(seq, _QKV_AB_LAYOUT)
    tables_c = _epilogue_tables(seq, _QKV_C_LAYOUT)
    sinks = sinks.astype(F32)
    lam_all = jnp.stack([lambda_q1, lambda_k1, lambda_q2, lambda_k2], axis=1).astype(F32)

    x = x.astype(F32).reshape(m, d_model)
    normed = _normed_operand(x, ffn1_norm, 0)
    for layer in range(depth):
        last = layer + 1 == depth
        w_next = {}

        def guests(*names):
            return () if last else tuple((f32_weights[name], layer + 1) for name in names)

        def hosted(names, cast):
            w_next.update(zip(names, cast))

        hidden, cast = _swiglu_up(normed, w["ffn1_w_in"], guests("ffn1_w_in"))
        hosted(("ffn1_w_in",), cast)
        x, normed, cast = _residual_proj(hidden, w["ffn1_w_out"], x, 0.5, mix_norm, layer, guests("ffn1_w_out"))
        hosted(("ffn1_w_out",), cast)

        qkv_ab, cast = _qkv_proj(normed, w["w_qkv"], _QKV_AB_LAYOUT, tables_ab, seq, guests("w_qkv"))
        hosted(("w_qkv",), cast)
        qkv_c, _ = _qkv_proj(normed, w["w_qkv"], _QKV_C_LAYOUT, tables_c, seq)
        out_a = _swa_attention(qkv_ab, sinks[layer], batch, seq)
        out_b = _stickbreak_attention(qkv_ab, batch, seq)
        lam_init = 0.8 - 0.6 * math.exp(-0.3 * layer)
        out_c = _diff_attention(qkv_c, lam_all[layer], diff_norm, layer, lam_init, batch, seq)
        merged, cast = _gated_merge(normed, w["w_gate"], out_a, w["w_branch_a"], out_b, w["w_branch_b"],
                                    out_c, w["w_branch_c"], guests("w_gate", "w_out"))
        hosted(("w_gate", "w_out"), cast)
        branch_names = ("w_branch_a", "w_branch_b", "w_branch_c")
        x, normed, cast = _residual_proj(merged, w["w_out"], x, 1.0, ffn2_norm, layer, guests(*branch_names))
        hosted(branch_names, cast)

        hidden, cast = _swiglu_up(normed, w["ffn2_w_in"], guests("ffn2_w_in"))
        hosted(("ffn2_w_in",), cast)
        if last:
            x, _, _ = _residual_proj(hidden, w["ffn2_w_out"], x, 0.5)
        else:
            x, normed, cast = _residual_proj(hidden, w["ffn2_w_out"], x, 0.5, ffn1_norm, layer + 1,
                                             guests("ffn2_w_out"))
            hosted(("ffn2_w_out",), cast)
        w = w_next

    return _rmsnorm(x, final_gain).reshape(batch, seq, d_model)
```

```python
import functools
import math

import jax
import jax.numpy as jnp
from jax import lax
from jax.experimental import pallas as pl
from jax.experimental.pallas import tpu as pltpu

F32 = jnp.float32
BF16 = jnp.bfloat16

HEAD_DIM = 128
ROPE_THETA = 10000.0
EPS = 1e-6
SWA_Q_HEADS = 16
SWA_KV_HEADS = 4
SWA_GROUP = SWA_Q_HEADS // SWA_KV_HEADS
WINDOW = 128
SB_HEADS = 8
DIFF_HEADS = 8
DIFF_QK_DIM = HEAD_DIM // 2
DIFF_NORM_EPS = 1e-5
QA_OFF = 0
KA_OFF = QA_OFF + SWA_Q_HEADS
VA_OFF = KA_OFF + SWA_KV_HEADS
QB_OFF = VA_OFF + SWA_KV_HEADS
KB_OFF = QB_OFF + SB_HEADS
VB_OFF = KB_OFF + SB_HEADS
QC_OFF = VB_OFF + SB_HEADS
KC_OFF = QC_OFF + DIFF_HEADS
VC_OFF = KC_OFF + DIFF_HEADS
QKV_HEADS = VC_OFF + DIFF_HEADS

V7X_VMEM_BYTES = 64 * 1024 * 1024
VMEM_LIMIT_BYTES = V7X_VMEM_BYTES * 3 // 4
LANES = 128
BF16_SUBLANES = 16
MASK_VALUE = -1e30
LOG2_E = math.log2(math.e)


def _params(n_grid_dims, sequential=0):
    semantics = ("parallel",) * (n_grid_dims - sequential) + ("arbitrary",) * sequential
    return pltpu.CompilerParams(dimension_semantics=semantics, vmem_limit_bytes=VMEM_LIMIT_BYTES)


def _resident_spec(block_shape, index_map):
    return pl.BlockSpec(block_shape, index_map, pipeline_mode=pl.Buffered(1))


def _inv_rms(ssq_ref, width, eps=EPS):
    return lax.rsqrt(ssq_ref[...] * (1.0 / width) + eps)


def _rmsnorm_body(x_ref, g_ref, o_ref, *, eps):
    x = x_ref[...]
    ms = jnp.mean(x * x, axis=-1, keepdims=True)
    o_ref[...] = (x * lax.rsqrt(ms + eps) * g_ref[...]).astype(o_ref.dtype)


def _rmsnorm(x, gain, *, block_rows=256):
    m, d = x.shape
    br = min(block_rows, m)
    return pl.pallas_call(
        functools.partial(_rmsnorm_body, eps=EPS),
        grid=(m // br,),
        in_specs=[pl.BlockSpec((br, d), lambda i: (i, 0)),
                  pl.BlockSpec((1, d), lambda i: (0, 0))],
        out_specs=pl.BlockSpec((br, d), lambda i: (i, 0)),
        out_shape=jax.ShapeDtypeStruct((m, d), F32),
        compiler_params=_params(1),
        name="rmsnorm",
    )(x, gain)


def _normed_operand_body(x_ref, g_ref, xg_ref, ssq_ref):
    x = x_ref[...]
    xg_ref[...] = (x * g_ref[...]).astype(xg_ref.dtype)
    ssq_ref[...] = jnp.sum(x * x, axis=-1, keepdims=True)


def _normed_operand(x, gains, layer, *, block_rows=256):
    m, d = x.shape
    br = min(block_rows, m)
    return pl.pallas_call(
        _normed_operand_body,
        grid=(m // br,),
        in_specs=[pl.BlockSpec((br, d), lambda i: (i, 0)),
                  pl.BlockSpec((None, 1, d), lambda i: (layer, 0, 0))],
        out_specs=[pl.BlockSpec((br, d), lambda i: (i, 0)),
                   pl.BlockSpec((br, 1), lambda i: (i, 0))],
        out_shape=[jax.ShapeDtypeStruct((m, d), BF16), jax.ShapeDtypeStruct((m, 1), F32)],
        compiler_params=_params(1),
        name="normed_operand",
    )(x, gains)


def _dense_call(body, grid, in_specs, out_specs, out_shapes, operands, guests, name, sequential=0):
    steps = grid[0] * grid[1]
    n_in, n_out, n_guests = len(in_specs), len(out_specs), len(guests)
    in_specs, out_specs, out_shapes, operands = list(in_specs), list(out_specs), list(out_shapes), list(operands)
    for weights, layer in guests:
        _, k, n = weights.shape
        rows = max(BF16_SUBLANES, pl.next_power_of_2(pl.cdiv(k, steps)))
        assert k % rows == 0
        last = k // rows - 1

        def block(i, j, last=last):
            return jnp.minimum(i * grid[1] + j, last)

        in_specs.append(pl.BlockSpec((None, rows, n),
                                     lambda i, j, layer=layer, block=block: (layer, block(i, j), 0)))
        out_specs.append(pl.BlockSpec((rows, n), lambda i, j, block=block: (block(i, j), 0)))
        out_shapes.append(jax.ShapeDtypeStruct((k, n), BF16))
        operands.append(weights)

    def full_body(*refs):
        ins, guest_ins = refs[:n_in], refs[n_in:n_in + n_guests]
        outs = refs[n_in + n_guests:n_in + n_guests + n_out]
        guest_outs = refs[n_in + n_guests + n_out:]
        for src, dst in zip(guest_ins, guest_outs):
            dst[...] = src[...].astype(dst.dtype)
        body(ins, outs)

    results = pl.pallas_call(
        full_body,
        grid=grid,
        in_specs=in_specs,
        out_specs=out_specs,
        out_shape=out_shapes,
        compiler_params=_params(2, sequential=2 if guests else sequential),
        name=name,
    )(*operands)
    return results[:n_out], results[n_out:]


def _swiglu_body(in_refs, out_refs):
    h_ref, ssq_ref, wg_ref, wu_ref = in_refs
    (o_ref,) = out_refs
    h = h_ref[...]
    inv = _inv_rms(ssq_ref, h.shape[1])
    g = jnp.dot(h, wg_ref[...], preferred_element_type=F32) * inv
    u = jnp.dot(h, wu_ref[...], preferred_element_type=F32) * inv
    o_ref[...] = (g * jax.nn.sigmoid(g) * u).astype(o_ref.dtype)


def _swiglu_up(normed, w_in, guests=(), *, bm=1024, bn=512):
    h, ssq = normed
    m, k = h.shape
    f = w_in.shape[1] // 2
    bm, bn = min(bm, m), min(bn, f)
    nj = f // bn
    (hidden,), cast = _dense_call(
        _swiglu_body, (m // bm, nj),
        [pl.BlockSpec((bm, k), lambda i, j: (i, 0)),
         pl.BlockSpec((bm, 1), lambda i, j: (i, 0)),
         pl.BlockSpec((k, bn), lambda i, j: (0, j)),
         pl.BlockSpec((k, bn), lambda i, j: (0, j + nj))],
        [pl.BlockSpec((bm, bn), lambda i, j: (i, j))],
        [jax.ShapeDtypeStruct((m, f), BF16)],
        (h, ssq, w_in, w_in), guests, "swiglu_up")
    return hidden, cast


def _residual_body(in_refs, out_refs, *, scale):
    a_ref, w_ref, r_ref = in_refs[:3]
    x = r_ref[...] + scale * jnp.dot(a_ref[...], w_ref[...], preferred_element_type=F32)
    out_refs[0][...] = x
    if len(out_refs) == 1:
        return
    g_ref = in_refs[3]
    _, xg_ref, ssq_ref = out_refs
    xg_ref[...] = (x * g_ref[...]).astype(xg_ref.dtype)

    @pl.when(pl.program_id(1) == 0)
    def _():
        ssq_ref[...] = jnp.zeros_like(ssq_ref)

    ssq_ref[...] += jnp.sum(x * x, axis=-1, keepdims=True)


def _residual_proj(a, w, resid, scale, next_gains=None, next_layer=None, guests=(), *, bm=1024, bn=512):
    m, k = a.shape
    n = w.shape[1]
    bm, bn = min(bm, m), min(bn, n)
    emit_normed = next_gains is not None
    tile = pl.BlockSpec((bm, bn), lambda i, j: (i, j))
    in_specs = [pl.BlockSpec((bm, k), lambda i, j: (i, 0)), pl.BlockSpec((k, bn), lambda i, j: (0, j)), tile]
    out_specs, out_shapes, operands = [tile], [jax.ShapeDtypeStruct((m, n), F32)], (a, w, resid)
    if emit_normed:
        in_specs.append(pl.BlockSpec((None, 1, bn), lambda i, j: (next_layer, 0, j)))
        out_specs += [tile, pl.BlockSpec((bm, 1), lambda i, j: (i, 0))]
        out_shapes += [jax.ShapeDtypeStruct((m, n), BF16), jax.ShapeDtypeStruct((m, 1), F32)]
        operands += (next_gains,)
    out, cast = _dense_call(functools.partial(_residual_body, scale=scale), (m // bm, n // bn), in_specs,
                            out_specs, out_shapes, operands, guests, "residual_proj",
                            sequential=int(emit_normed))
    return out[0], ((out[1], out[2]) if emit_normed else None), cast


_SCALE_FULL = LOG2_E / math.sqrt(HEAD_DIM)
_SCALE_HALF = LOG2_E / math.sqrt(DIFF_QK_DIM)
_QKV_AB_LAYOUT = ((QA_OFF, SWA_Q_HEADS, HEAD_DIM, _SCALE_FULL), (KA_OFF, SWA_KV_HEADS, HEAD_DIM, 1.0),
                  (VA_OFF, SWA_KV_HEADS, None, 1.0), (QB_OFF, SB_HEADS, None, _SCALE_FULL),
                  (KB_OFF, 2 * SB_HEADS, None, 1.0))
_QKV_C_LAYOUT = ((QC_OFF, DIFF_HEADS, DIFF_QK_DIM, _SCALE_HALF), (KC_OFF, DIFF_HEADS, DIFF_QK_DIM, 1.0),
                 (VC_OFF, DIFF_HEADS, None, 1.0))


def _epilogue_tables(seq, layout):
    mult, rot = [], []
    for _, _, rope_dim, scale in layout:
        if rope_dim is None:
            mult.append(jnp.full((seq, HEAD_DIM), scale, F32))
            rot.append(jnp.zeros((seq, HEAD_DIM), F32))
            continue
        inv = 1.0 / (ROPE_THETA ** (jnp.arange(0, rope_dim, 2, dtype=F32) / rope_dim))
        ang = jnp.arange(seq, dtype=F32)[:, None] * inv[None, :]
        cos, sin = jnp.cos(ang), jnp.sin(ang)
        reps = HEAD_DIM // rope_dim
        mult.append(jnp.tile(jnp.concatenate([cos, cos], axis=-1), (1, reps)) * scale)
        rot.append(jnp.tile(jnp.concatenate([-sin, sin], axis=-1), (1, reps)) * scale)
    return jnp.stack(mult), jnp.stack(rot)


def _qkv_body(in_refs, out_refs, *, rope_dim, heads_per_block):
    h_ref, ssq_ref, w_ref, mult_ref, rot_ref = in_refs
    (o_ref,) = out_refs
    h = h_ref[...]
    acc = jnp.dot(h, w_ref[...], preferred_element_type=F32) * _inv_rms(ssq_ref, h.shape[1])

    def partner(x):
        if rope_dim == HEAD_DIM:
            return pltpu.roll(x, HEAD_DIM // 2, axis=1)
        lane = lax.broadcasted_iota(jnp.int32, x.shape, 1)
        return jnp.where((lane % rope_dim) < (rope_dim // 2),
                         pltpu.roll(x, HEAD_DIM - rope_dim // 2, axis=1),
                         pltpu.roll(x, rope_dim // 2, axis=1))

    for s in range(heads_per_block):
        cols = slice(s * HEAD_DIM, (s + 1) * HEAD_DIM)
        x = acc[:, cols]
        o_ref[:, cols] = (x * mult_ref[...] + partner(x) * rot_ref[...]).astype(o_ref.dtype)


def _qkv_proj(normed, w_qkv, layout, tables, seq, guests=(), *, bm=1024, max_bn=1024):
    h, ssq = normed
    mult, rot = tables
    m, k = h.shape
    bm = min(bm, seq)
    hpb = math.gcd(max_bn // HEAD_DIM, *(count for _, count, _, _ in layout))
    bn = hpb * HEAD_DIM
    first_head = layout[0][0]
    n_heads = sum(count for _, count, _, _ in layout)
    rope_dims = {rope_dim for _, _, rope_dim, _ in layout if rope_dim is not None}
    assert len(rope_dims) == 1 and first_head % hpb == 0
    block_segment = []
    for seg, (_, count, _, _) in enumerate(layout):
        assert count % hpb == 0
        block_segment += [seg] * (count // hpb)

    def segment_of(j):
        return sum(jnp.where(j == blk, seg, 0) for blk, seg in enumerate(block_segment) if seg)

    pos_blocks = seq // bm
    table_spec = pl.BlockSpec((None, bm, LANES), lambda i, j: (segment_of(j), i % pos_blocks, 0))
    first_block = first_head // hpb
    (qkv,), cast = _dense_call(
        functools.partial(_qkv_body, rope_dim=rope_dims.pop(), heads_per_block=hpb),
        (m // bm, n_heads // hpb),
        [pl.BlockSpec((bm, k), lambda i, j: (i, 0)),
         pl.BlockSpec((bm, 1), lambda i, j: (i, 0)),
         pl.BlockSpec((k, bn), lambda i, j: (0, first_block + j)),
         table_spec, table_spec],
        [pl.BlockSpec((bm, bn), lambda i, j: (i, j))],
        [jax.ShapeDtypeStruct((m, n_heads * HEAD_DIM), BF16)],
        (h, ssq, w_qkv, mult, rot), guests, "qkv_proj")
    return qkv, cast


def _swa_body(sinks_ref, q_ref, kp_ref, kc_ref, vp_ref, vc_ref, o_ref):
    i = pl.program_id(1)
    blk = WINDOW
    rows = SWA_GROUP * blk
    t = lax.broadcasted_iota(jnp.int32, (rows, 2 * blk), 0) % blk
    c = lax.broadcasted_iota(jnp.int32, (rows, 2 * blk), 1)
    valid = (c > t) & (c <= t + blk) & ((c >= blk) | (i > 0))
    group = lax.broadcasted_iota(jnp.int32, (rows, 1), 0) // blk

    def head_cols(h):
        return slice(h * HEAD_DIM, (h + 1) * HEAD_DIM)

    scores = []
    for hk in range(SWA_KV_HEADS):
        qs = jnp.concatenate([q_ref[:, head_cols(hk * SWA_GROUP + g)] for g in range(SWA_GROUP)], axis=0)
        k = jnp.concatenate([kp_ref[:, head_cols(hk)], kc_ref[:, head_cols(hk)]], axis=0)
        scores.append(lax.dot_general(qs, k, (((1,), (1,)), ((), ())), preferred_element_type=F32))
    for hk in range(SWA_KV_HEADS):
        s = jnp.where(valid, scores[hk], MASK_VALUE)
        sink = jnp.zeros((rows, 1), F32)
        for g in range(SWA_GROUP):
            sink = jnp.where(group == g, sinks_ref[hk * SWA_GROUP + g] * LOG2_E, sink)
        m = jnp.maximum(jnp.max(s, axis=-1, keepdims=True), sink)
        p = jnp.exp2(s - m)
        denom = jnp.sum(p, axis=-1, keepdims=True) + jnp.exp2(sink - m)
        v = jnp.concatenate([vp_ref[:, head_cols(hk)], vc_ref[:, head_cols(hk)]], axis=0)
        o = jnp.dot(p.astype(BF16), v, preferred_element_type=F32) / denom
        for g in range(SWA_GROUP):
            o_ref[:, head_cols(hk * SWA_GROUP + g)] = o[g * blk:(g + 1) * blk].astype(o_ref.dtype)


def _swa_attention(qkv, sinks, batch, seq):
    m = qkv.shape[0]
    blk = WINDOW
    nq = seq // blk
    qw = SWA_Q_HEADS * HEAD_DIM
    kw = SWA_KV_HEADS * HEAD_DIM
    assert QA_OFF % SWA_Q_HEADS == 0 and KA_OFF % SWA_KV_HEADS == 0 and VA_OFF % SWA_KV_HEADS == 0

    def cur(off):
        return pl.BlockSpec((blk, kw), lambda b, i: (b * nq + i, off // SWA_KV_HEADS))

    def prev(off):
        return pl.BlockSpec((blk, kw), lambda b, i: (b * nq + jnp.maximum(i - 1, 0), off // SWA_KV_HEADS))

    return pl.pallas_call(
        _swa_body,
        grid=(batch, nq),
        in_specs=[pl.BlockSpec(memory_space=pltpu.SMEM),
                  pl.BlockSpec((blk, qw), lambda b, i: (b * nq + i, QA_OFF // SWA_Q_HEADS)),
                  prev(KA_OFF), cur(KA_OFF), prev(VA_OFF), cur(VA_OFF)],
        out_specs=pl.BlockSpec((blk, qw), lambda b, i: (b * nq + i, 0)),
        out_shape=jax.ShapeDtypeStruct((m, qw), BF16),
        compiler_params=_params(2),
        name="swa_attention",
    )(sinks, qkv, qkv, qkv, qkv, qkv)


def _neg_abs(x):
    bits = lax.bitcast_convert_type(x, jnp.int32) | jnp.int32(-2 ** 31)
    return lax.bitcast_convert_type(bits, F32)


def _stickbreak_body(q_ref, k_ref, v_ref, o_ref, vt_ref, acc_ref, *, blk, heads):
    i = pl.program_id(2)
    nkb = k_ref.shape[0] // blk
    head_cols = [slice(c * HEAD_DIM, (c + 1) * HEAD_DIM) for c in range(heads)]

    @pl.when(i == 0)
    def _():
        for c in range(heads):
            for kb in range(nkb):
                vt_ref[c, kb] = v_ref[kb * blk:(kb + 1) * blk, head_cols[c]].T

    row = lax.broadcasted_iota(jnp.int32, (blk, blk), 0)
    col = lax.broadcasted_iota(jnp.int32, (blk, blk), 1)
    from_key = jnp.where(col >= row, 1.0, 0.0).astype(BF16)
    causal = row < col

    def tile(kb, run, diagonal):
        rows = pl.ds(pl.multiple_of(kb * blk, blk), blk)
        z = [lax.dot_general(k_ref[rows, cols], q_ref[:, cols], (((1,), (1,)), ((), ())),
                             preferred_element_type=F32) for cols in head_cols]
        suffix = []
        for c in range(heads):
            softplus = jnp.maximum(z[c], 0.0) + jnp.log2(1.0 + jnp.exp2(_neg_abs(z[c])))
            if diagonal:
                softplus = jnp.where(causal, softplus, 0.0)
            hi = softplus.astype(BF16)
            lo = (softplus - hi.astype(F32)).astype(BF16)
            suffix.append(jnp.dot(from_key, hi, preferred_element_type=F32)
                          + jnp.dot(from_key, lo, preferred_element_type=F32))
        new_run = []
        for c in range(heads):
            a = jnp.exp2(z[c] - (run[c] + suffix[c]))
            if diagonal:
                a = jnp.where(causal, a, 0.0)
            pv = jnp.dot(vt_ref[c, kb], a.astype(BF16), preferred_element_type=F32)
            acc_ref[c] = pv if diagonal else acc_ref[c] + pv
            new_run.append(run[c] + suffix[c][0:1])
        return tuple(new_run)

    run = tile(i, tuple(jnp.zeros((1, blk), F32) for _ in range(heads)), True)
    lax.fori_loop(0, i, lambda t, r: tile(i - 1 - t, r, False), run)
    for c in range(heads):
        o_ref[:, head_cols[c]] = acc_ref[c].T.astype(o_ref.dtype)


def _stickbreak_attention(qkv, batch, seq, *, blk=256, heads=8):
    m = qkv.shape[0]
    blk = min(blk, seq)
    nq = seq // blk
    width = heads * HEAD_DIM
    assert all(off % heads == 0 for off in (QB_OFF, KB_OFF, VB_OFF, SB_HEADS))
    return pl.pallas_call(
        functools.partial(_stickbreak_body, blk=blk, heads=heads),
        grid=(batch, SB_HEADS // heads, nq),
        in_specs=[pl.BlockSpec((blk, width), lambda b, h, i: (b * nq + i, QB_OFF // heads + h)),
                  _resident_spec((seq, width), lambda b, h, i: (b, KB_OFF // heads + h)),
                  _resident_spec((seq, width), lambda b, h, i: (b, VB_OFF // heads + h))],
        out_specs=pl.BlockSpec((blk, width), lambda b, h, i: (b * nq + i, h)),
        out_shape=jax.ShapeDtypeStruct((m, SB_HEADS * HEAD_DIM), BF16),
        scratch_shapes=[pltpu.VMEM((heads, nq, HEAD_DIM, blk), BF16),
                        pltpu.VMEM((heads, HEAD_DIM, blk), F32)],
        compiler_params=_params(3, sequential=1),
        name="stickbreak_attention",
    )(qkv, qkv, qkv)


def _diff_body(q_ref, k_ref, v_ref, lam_ref, g_ref, o_ref, vt_ref, acc_ref, *, blk, heads, lam_init):
    i = pl.program_id(2)
    nkb = k_ref.shape[0] // blk
    head_cols = [slice(h * HEAD_DIM, (h + 1) * HEAD_DIM) for h in range(heads)]

    @pl.when(i == 0)
    def _():
        for h in range(heads):
            for kb in range(nkb):
                vt_ref[h, kb] = v_ref[kb * blk:(kb + 1) * blk, head_cols[h]].T

    lane = lax.broadcasted_iota(jnp.int32, (blk, HEAD_DIM), 1)
    key = lax.broadcasted_iota(jnp.int32, (blk, 2 * blk), 0)
    qry = lax.broadcasted_iota(jnp.int32, (blk, 2 * blk), 1) % blk
    causal = key <= qry

    def both_maps(q):
        zero = jnp.zeros_like(q)
        return jnp.concatenate([jnp.where(lane < DIFF_QK_DIM, q, zero),
                                jnp.where(lane >= DIFF_QK_DIM, q, zero)], axis=0)

    qq = [both_maps(q_ref[:, cols]) for cols in head_cols]

    def tile(kb, state, diagonal):
        rows = pl.ds(pl.multiple_of(kb * blk, blk), blk)
        scores = [lax.dot_general(k_ref[rows, head_cols[h]], qq[h], (((1,), (1,)), ((), ())),
                                  preferred_element_type=F32) for h in range(heads)]
        new_state = []
        for h in range(heads):
            m, l = state[h]
            s = jnp.where(causal, scores[h], MASK_VALUE) if diagonal else scores[h]
            m_new = jnp.maximum(m, jnp.max(s, axis=0, keepdims=True))
            alpha = jnp.exp2(m - m_new)
            p = jnp.exp2(s - m_new)
            pv = jnp.dot(vt_ref[h, kb], p.astype(BF16), preferred_element_type=F32)
            acc_ref[h] = pv if diagonal else alpha * acc_ref[h] + pv
            new_state.append((m_new, alpha * l + jnp.sum(p, axis=0, keepdims=True)))
        return tuple(new_state)

    state = tuple((jnp.full((1, 2 * blk), MASK_VALUE, F32), jnp.zeros((1, 2 * blk), F32))
                  for _ in range(heads))
    state = tile(i, state, True)
    state = lax.fori_loop(0, i, lambda kb, s: tile(kb, s, False), state)

    lam_vecs = lam_ref[...]
    lam = (jnp.exp(jnp.sum(lam_vecs[0:1] * lam_vecs[1:2], axis=-1, keepdims=True))
           - jnp.exp(jnp.sum(lam_vecs[2:3] * lam_vecs[3:4], axis=-1, keepdims=True))
           + lam_init)
    for h in range(heads):
        o = acc_ref[h] / state[h][1]
        d = o[:, :blk] - lam * o[:, blk:]
        ms = jnp.mean(d * d, axis=0, keepdims=True)
        y = d * lax.rsqrt(ms + DIFF_NORM_EPS) * g_ref[...]
        o_ref[:, head_cols[h]] = (y * (1.0 - lam_init)).T.astype(o_ref.dtype)


def _diff_attention(qkv, lam_vecs, diff_norm, layer, lam_init, batch, seq, *, blk=512, heads=4):
    m = qkv.shape[0]
    blk = min(blk, seq)
    nq = seq // blk
    width = heads * HEAD_DIM
    q_blk, k_blk, v_blk = ((off - QC_OFF) // heads for off in (QC_OFF, KC_OFF, VC_OFF))
    assert DIFF_HEADS % heads == 0
    return pl.pallas_call(
        functools.partial(_diff_body, blk=blk, heads=heads, lam_init=lam_init),
        grid=(batch, DIFF_HEADS // heads, nq),
        in_specs=[pl.BlockSpec((blk, width), lambda b, h, i: (b * nq + i, q_blk + h)),
                  _resident_spec((seq, width), lambda b, h, i: (b, k_blk + h)),
                  _resident_spec((seq, width), lambda b, h, i: (b, v_blk + h)),
                  pl.BlockSpec(lam_vecs.shape, lambda b, h, i: (0, 0)),
                  pl.BlockSpec((None, HEAD_DIM, 1), lambda b, h, i: (layer, 0, 0))],
        out_specs=pl.BlockSpec((blk, width), lambda b, h, i: (b * nq + i, h)),
        out_shape=jax.ShapeDtypeStruct((m, DIFF_HEADS * HEAD_DIM), BF16),
        scratch_shapes=[pltpu.VMEM((heads, nq, HEAD_DIM, blk), BF16),
                        pltpu.VMEM((heads, HEAD_DIM, 2 * blk), F32)],
        compiler_params=_params(3, sequential=1),
        name="diff_attention",
    )(qkv, qkv, qkv, lam_vecs, diff_norm)


def _merge_body(in_refs, out_refs):
    h_ref, ssq_ref, wga_ref, wgb_ref, wgc_ref, oa_ref, wa_ref, ob_ref, wb_ref, oc_ref, wc_ref = in_refs
    (o_ref,) = out_refs
    h = h_ref[...]
    inv = _inv_rms(ssq_ref, h.shape[1])

    def gated(wg_ref, x_ref, w_ref):
        gate = jax.nn.sigmoid(jnp.dot(h, wg_ref[...], preferred_element_type=F32) * inv)
        return gate * jnp.dot(x_ref[...], w_ref[...], preferred_element_type=F32)

    merged = gated(wga_ref, oa_ref, wa_ref) + gated(wgb_ref, ob_ref, wb_ref) + gated(wgc_ref, oc_ref, wc_ref)
    o_ref[...] = merged.astype(o_ref.dtype)


def _gated_merge(normed, w_gate, out_a, w_a, out_b, w_b, out_c, w_c, guests=(), *, bm=1024, bn=256):
    h, ssq = normed
    m, k = h.shape
    d = w_a.shape[1]
    bm, bn = min(bm, m), min(bn, d)
    nj = d // bn

    def act(x):
        return _resident_spec((bm, x.shape[1]), lambda i, j: (i, 0))

    def weight(w, block_off=0):
        return pl.BlockSpec((w.shape[0], bn), lambda i, j: (0, j + block_off))

    (merged,), cast = _dense_call(
        _merge_body, (m // bm, nj),
        [act(h), act(ssq), weight(w_gate), weight(w_gate, nj), weight(w_gate, 2 * nj),
         act(out_a), weight(w_a), act(out_b), weight(w_b), act(out_c), weight(w_c)],
        [pl.BlockSpec((bm, bn), lambda i, j: (i, j))],
        [jax.ShapeDtypeStruct((m, d), BF16)],
        (h, ssq, w_gate, w_gate, w_gate, out_a, w_a, out_b, w_b, out_c, w_c), guests, "gated_merge")
    return merged, cast


def kernel(x, ffn1_norm, ffn1_w_in, ffn1_w_out, mix_norm, w_qkv, w_gate, sinks, lambda_q1, lambda_k1, lambda_q2, lambda_k2, diff_norm, w_branch_a, w_branch_b, w_branch_c, w_out, ffn2_norm, ffn2_w_in, ffn2_w_out, final_norm):
    batch, seq, d_model = x.shape
    depth = ffn1_w_in.shape[0]
    assert w_qkv.shape[2] == QKV_HEADS * HEAD_DIM
    m = batch * seq

    def gains(g):
        return g.astype(F32).reshape(g.shape[0], 1, g.shape[1])

    f32_weights = dict(ffn1_w_in=ffn1_w_in, ffn1_w_out=ffn1_w_out, w_qkv=w_qkv, w_gate=w_gate, w_out=w_out,
                       w_branch_a=w_branch_a, w_branch_b=w_branch_b, w_branch_c=w_branch_c,
                       ffn2_w_in=ffn2_w_in, ffn2_w_out=ffn2_w_out)
    f32_weights = {name: w.astype(F32) for name, w in f32_weights.items()}
    w = {name: full[0].astype(BF16) for name, full in f32_weights.items()}
    ffn1_norm, mix_norm, ffn2_norm = map(gains, (ffn1_norm, mix_norm, ffn2_norm))
    diff_norm = diff_norm.astype(F32).reshape(depth, HEAD_DIM, 1)
    final_gain = final_norm.astype(F32).reshape(1, d_model)
    tables_ab = _epilogue_tables(seq, _QKV_AB_LAYOUT)
    tables_c = _epilogue_tables(seq, _QKV_C_LAYOUT)
    sinks = sinks.astype(F32)
    lam_all = jnp.stack([lambda_q1, lambda_k1, lambda_q2, lambda_k2], axis=1).astype(F32)

    x = x.astype(F32).reshape(m, d_model)
    normed = _normed_operand(x, ffn1_norm, 0)
    for layer in range(depth):
        last = layer + 1 == depth
        w_next = {}

        def guests(*names):
            return () if last else tuple((f32_weights[name], layer + 1) for name in names)

        def hosted(names, cast):
            w_next.update(zip(names, cast))

        hidden, cast = _swiglu_up(normed, w["ffn1_w_in"], guests("ffn1_w_in"))
        hosted(("ffn1_w_in",), cast)
        x, normed, cast = _residual_proj(hidden, w["ffn1_w_out"], x, 0.5, mix_norm, layer, guests("ffn1_w_out"))
        hosted(("ffn1_w_out",), cast)

        qkv_ab, cast = _qkv_proj(normed, w["w_qkv"], _QKV_AB_LAYOUT, tables_ab, seq, guests("w_qkv"))
        hosted(("w_qkv",), cast)
        qkv_c, _ = _qkv_proj(normed, w["w_qkv"], _QKV_C_LAYOUT, tables_c, seq)
        out_a = _swa_attention(qkv_ab, sinks[layer], batch, seq)
        out_b = _stickbreak_attention(qkv_ab, batch, seq)
        lam_init = 0.8 - 0.6 * math.exp(-0.3 * layer)
        out_c = _diff_attention(qkv_c, lam_all[layer], diff_norm, layer, lam_init, batch, seq)
        merged, cast = _gated_merge(normed, w["w_gate"], out_a, w["w_branch_a"], out_b, w["w_branch_b"],
                                    out_c, w["w_branch_c"], guests("w_gate", "w_out"))
        hosted(("w_gate", "w_out"), cast)
        branch_names = ("w_branch_a", "w_branch_b", "w_branch_c")
        x, normed, cast = _residual_proj(merged, w["w_out"], x, 1.0, ffn2_norm, layer, guests(*branch_names))
        hosted(branch_names, cast)

        hidden, cast = _swiglu_up(normed, w["ffn2_w_in"], guests("ffn2_w_in"))
        hosted(("ffn2_w_in",), cast)
        if last:
            x, _, _ = _residual_proj(hidden, w["ffn2_w_out"], x, 0.5)
        else:
            x, normed, cast = _residual_proj(hidden, w["ffn2_w_out"], x, 0.5, ffn1_norm, layer + 1,
                                             guests("ffn2_w_out"))
            hosted(("ffn2_w_out",), cast)
        w = w_next

    return _rmsnorm(x, final_gain).reshape(batch, seq, d_model)
```
